```python
import jax, jax.numpy as jnp
from jax import lax
import numpy as np


D_MODEL = 1024
BATCH = 4
SEQ = 8192
DEPTH = 2

CONV_CH = D_MODEL // 2
CONV_K = 3
N_HEADS = 8
HEAD_DIM = 64
ATTN_W = N_HEADS * HEAD_DIM
Q_BLOCK = 128
IN_WIDTH = 3 * CONV_CH + 3 * ATTN_W + 2 * D_MODEL
N_EXPERTS = 16
N_GROUPS = 4
EXPERTS_PER_GROUP = N_EXPERTS // N_GROUPS
TOP_K = 2
EXPERT_FF = 512
MOE_BLOCK = 128
LN_EPS = 1e-5
DN_ALPHA = (2 * DEPTH) ** 0.25
DN_BETA = (8 * DEPTH) ** -0.25

kernel_name = 'hybrid_conv_stickbreak_groupmoe_deepnorm_adaln'


def layer_norm(h, g, b):
    hf = h.astype(jnp.float32)
    mu = jnp.mean(hf, axis=-1, keepdims=True)
    var = jnp.mean(jnp.square(hf - mu), axis=-1, keepdims=True)
    y = (hf - mu) * lax.rsqrt(var + LN_EPS) * g.astype(jnp.float32) + b.astype(jnp.float32)
    return y.astype(h.dtype)


def causal_dwconv(h, w):
    S = h.shape[1]
    hp = jnp.pad(h, ((0, 0), (CONV_K - 1, 0), (0, 0)))
    out = w[CONV_K - 1] * h
    for j in range(CONV_K - 1):
        out = out + w[j] * hp[:, j:j + S]
    return out


def stick_breaking_attention(q, k, v):
    S = q.shape[2]
    scale = HEAD_DIM ** -0.5
    outs = []
    for i in range(S // Q_BLOCK):
        q0 = i * Q_BLOCK
        end = q0 + Q_BLOCK
        qb = q[:, :, q0:end]
        kb = k[:, :, :end]
        vb = v[:, :, :end]
        z = jnp.einsum('bhqd,bhkd->bhqk', qb, kb, preferred_element_type=jnp.float32) * scale
        t_idx = q0 + jnp.arange(Q_BLOCK)[:, None]
        s_idx = jnp.arange(end)[None, :]
        mask = s_idx < t_idx
        log1m = jnp.where(mask, jax.nn.log_sigmoid(-z), 0.0)
        between = lax.cumsum(log1m, axis=3, reverse=True) - log1m
        attn = jnp.where(mask, jnp.exp(jax.nn.log_sigmoid(z) + between), 0.0)
        outs.append(jnp.einsum('bhqk,bhkd->bhqd', attn.astype(vb.dtype), vb))
    return jnp.concatenate(outs, axis=2)


def token_mixer(u, w_in, w_conv, w_a, w_b, w_o):
    Bn, Sn, _ = u.shape
    proj = u @ w_in
    cuts = np.cumsum([CONV_CH, CONV_CH, CONV_CH, ATTN_W, ATTN_W, ATTN_W, D_MODEL])
    cb, cc, cx, q, k, v, ga, gb = jnp.split(proj, cuts, axis=-1)
    y_a = (cb * causal_dwconv(cc * cx, w_conv)) @ w_a
    heads = lambda t: t.reshape(Bn, Sn, N_HEADS, HEAD_DIM).transpose(0, 2, 1, 3)
    o = stick_breaking_attention(heads(q), heads(k), heads(v))
    y_b = o.transpose(0, 2, 1, 3).reshape(Bn, Sn, ATTN_W) @ w_b
    merged = jax.nn.sigmoid(ga) * y_a + jax.nn.sigmoid(gb) * y_b
    return merged @ w_o


def moe_ffn(u, w_router, b_router, w_gate, w_up, w_down):
    Bn, Sn, D = u.shape
    N = Bn * Sn
    xf = u.reshape(N, D)
    aff = jax.nn.sigmoid((xf @ w_router).astype(jnp.float32))
    sel = (aff + b_router.astype(jnp.float32)).reshape(N, N_GROUPS, EXPERTS_PER_GROUP)
    grp_score = lax.top_k(sel, TOP_K)[0].sum(-1)
    g_idx = jnp.argmax(grp_score, axis=-1)
    in_grp = jnp.take_along_axis(sel, g_idx[:, None, None], axis=1)[:, 0]
    _, local = lax.top_k(in_grp, TOP_K)
    e_idx = g_idx[:, None] * EXPERTS_PER_GROUP + local
    gates = jnp.take_along_axis(aff, e_idx, axis=1)
    gates = gates / jnp.sum(gates, axis=-1, keepdims=True)
    A = N * TOP_K
    flat_e = e_idx.reshape(A).astype(jnp.int32)
    flat_tok = jnp.repeat(jnp.arange(N, dtype=jnp.int32), TOP_K)
    flat_w = gates.reshape(A)
    order = jnp.argsort(flat_e)
    s_e = flat_e[order]
    s_tok = flat_tok[order]
    s_w = flat_w[order]
    counts = jnp.bincount(flat_e, length=N_EXPERTS)
    pad_counts = (counts + MOE_BLOCK - 1) // MOE_BLOCK * MOE_BLOCK
    pad_end = jnp.cumsum(pad_counts)
    pad_start = pad_end - pad_counts
    start = jnp.cumsum(counts) - counts
    dest = pad_start[s_e] + (jnp.arange(A, dtype=jnp.int32) - start[s_e])
    n_blocks = -(-A // MOE_BLOCK) + N_EXPERTS
    P = n_blocks * MOE_BLOCK
    row_tok = jnp.full((P,), N, dtype=jnp.int32).at[dest].set(s_tok)
    x_ext = jnp.concatenate([xf, jnp.zeros((1, D), xf.dtype)], axis=0)
    x_blocks = x_ext[row_tok].reshape(n_blocks, MOE_BLOCK, D)
    block_start = jnp.arange(n_blocks, dtype=jnp.int32) * MOE_BLOCK
    block_exp = jnp.clip(jnp.searchsorted(pad_end, block_start, side='right'), 0, N_EXPERTS - 1)

    def expert_block(args):
        xb, e = args
        h = jax.nn.silu(xb @ w_gate[e]) * (xb @ w_up[e])
        return h @ w_down[e]

    y_blocks = lax.map(expert_block, (x_blocks, block_exp))
    y_sorted = y_blocks.reshape(P, D)[dest]
    out = jax.ops.segment_sum(y_sorted.astype(jnp.float32) * s_w[:, None], s_tok, num_segments=N)
    return out.astype(u.dtype).reshape(Bn, Sn, D)


def setup_inputs(seed: int = 0) -> dict:
    key = jax.random.key(seed)
    ks = jax.random.split(key, 20)
    nrm = lambda k, shape: jax.random.normal(k, shape, jnp.float32)
    col_scale = jnp.concatenate([
        jnp.ones((2 * CONV_CH,), jnp.float32), jnp.full((CONV_CH,), DN_BETA, jnp.float32),
        jnp.ones((2 * ATTN_W,), jnp.float32), jnp.full((ATTN_W,), DN_BETA, jnp.float32),
        jnp.ones((2 * D_MODEL,), jnp.float32)])
    return {
        'x': nrm(ks[0], (BATCH, SEQ, D_MODEL)),
        'c': nrm(ks[1], (BATCH, D_MODEL)),
        'w_ada': nrm(ks[2], (DEPTH, D_MODEL, 6 * D_MODEL)) * D_MODEL ** -0.5,
        'b_ada': 0.02 * nrm(ks[3], (DEPTH, 6 * D_MODEL)),
        'w_in': nrm(ks[4], (DEPTH, D_MODEL, IN_WIDTH)) * D_MODEL ** -0.5 * col_scale,
        'w_conv': nrm(ks[5], (DEPTH, CONV_K, CONV_CH)) * CONV_K ** -0.5,
        'w_a': nrm(ks[6], (DEPTH, CONV_CH, D_MODEL)) * CONV_CH ** -0.5,
        'w_b': nrm(ks[7], (DEPTH, ATTN_W, D_MODEL)) * ATTN_W ** -0.5,
        'w_o': nrm(ks[8], (DEPTH, D_MODEL, D_MODEL)) * D_MODEL ** -0.5 * DN_BETA,
        'ln1_g': 1.0 + 0.02 * nrm(ks[9], (DEPTH, D_MODEL)),
        'ln1_b': 0.02 * nrm(ks[10], (DEPTH, D_MODEL)),
        'w_router': nrm(ks[11], (D_MODEL, N_EXPERTS)) * D_MODEL ** -0.5,
        'b_router': 0.01 * nrm(ks[12], (N_EXPERTS,)),
        'w_gate': nrm(ks[13], (DEPTH, N_EXPERTS, D_MODEL, EXPERT_FF)) * D_MODEL ** -0.5,
        'w_up': nrm(ks[14], (DEPTH, N_EXPERTS, D_MODEL, EXPERT_FF)) * D_MODEL ** -0.5,
        'w_down': nrm(ks[15], (DEPTH, N_EXPERTS, EXPERT_FF, D_MODEL)) * EXPERT_FF ** -0.5 * DN_BETA,
        'ln2_g': 1.0 + 0.02 * nrm(ks[16], (DEPTH, D_MODEL)),
        'ln2_b': 0.02 * nrm(ks[17], (DEPTH, D_MODEL)),
    }


def reference(x, c, w_ada, b_ada, w_in, w_conv, w_a, w_b, w_o, ln1_g, ln1_b,
              w_router, b_router, w_gate, w_up, w_down, ln2_g, ln2_b):
    for l in range(DEPTH):
        mod = jax.nn.silu(c) @ w_ada[l] + b_ada[l]
        sh1, sc1, g1, sh2, sc2, g2 = [m[:, None, :] for m in jnp.split(mod, 6, axis=-1)]
        u = x * (1.0 + sc1) + sh1
        mix = token_mixer(u, w_in[l], w_conv[l], w_a[l], w_b[l], w_o[l])
        x = layer_norm(DN_ALPHA * x + g1 * mix, ln1_g[l], ln1_b[l])
        u2 = x * (1.0 + sc2) + sh2
        ffn = moe_ffn(u2, w_router, b_router, w_gate[l], w_up[l], w_down[l])
        x = layer_norm(DN_ALPHA * x + g2 * ffn, ln2_g[l], ln2_b[l])
    return x
```

```python
import functools

import jax
import jax.numpy as jnp
from jax import lax
from jax.experimental import pallas as pl
from jax.experimental.pallas import tpu as pltpu

F32 = jnp.float32
BF16 = jnp.bfloat16

N_HEADS = 8
HEAD_DIM = 64
ATTN_W = N_HEADS * HEAD_DIM
N_EXPERTS = 16
N_GROUPS = 4
EXPERTS_PER_GROUP = N_EXPERTS // N_GROUPS
TOP_K = 2
CONV_K = 3
LN_EPS = 1e-5

T_INPROJ = 512
T_POST = 256
TQ = 256
TK = 256
TM = 256
ADA_TN = 1536

VMEM_LIMIT = 56 * 1024 * 1024


def _cparams(sem):
    return pltpu.CompilerParams(dimension_semantics=sem, vmem_limit_bytes=VMEM_LIMIT)


def _dot(a, b):
    return jnp.dot(a, b, preferred_element_type=F32)


def _dot_nt(a, b):
    return lax.dot_general(a, b, (((1,), (1,)), ((), ())), preferred_element_type=F32)


def _split_bf16(v):
    hi = v.astype(BF16)
    lo = (v - hi.astype(F32)).astype(BF16)
    return hi, lo


def _ada_kernel(c_ref, w_ref, b_ref, o_ref):
    c = c_ref[...]
    s = c * jax.nn.sigmoid(c)
    o_ref[0] = jnp.dot(s, w_ref[0], precision=lax.Precision.HIGHEST, preferred_element_type=F32) + b_ref[0]


def _ada(c_pad, w_ada, b_ada):
    depth, d, n6 = w_ada.shape
    rows = c_pad.shape[0]
    return pl.pallas_call(
        _ada_kernel,
        grid=(depth, n6 // ADA_TN),
        in_specs=[
            pl.BlockSpec((rows, d), lambda l, j: (0, 0)),
            pl.BlockSpec((1, d, ADA_TN), lambda l, j: (l, 0, j)),
            pl.BlockSpec((1, 1, ADA_TN), lambda l, j: (l, 0, j)),
        ],
        out_specs=pl.BlockSpec((1, rows, ADA_TN), lambda l, j: (l, 0, j)),
        out_shape=jax.ShapeDtypeStruct((depth, rows, n6), F32),
        compiler_params=_cparams(("arbitrary", "arbitrary")),
        name="adaln",
    )(c_pad, w_ada, b_ada.reshape(depth, 1, n6))


def _inproj_kernel(x_ref, mod_ref, wc_ref, wqkv_ref, wconv_ref, g_ref, q_ref, k_ref, v_ref, carry_ref):
    j = pl.program_id(1)
    t = x_ref.shape[1]
    cch = wconv_ref.shape[1]

    @pl.when(j == 0)
    def _():
        carry_ref[...] = jnp.zeros_like(carry_ref)

    sh1 = mod_ref[0, 0:1, :]
    sc1 = mod_ref[0, 1:2, :]
    u = (x_ref[0] * (1.0 + sc1) + sh1).astype(BF16)

    pc = _dot(u, wc_ref[...])
    cb = pc[:, 0:cch]
    h = pc[:, cch:2 * cch] * pc[:, 2 * cch:3 * cch]
    prev = carry_ref[...]
    p1 = prev[7:8, :]
    p2 = prev[6:7, :]
    row = lax.broadcasted_iota(jnp.int32, h.shape, 0)
    hm1 = jnp.where(row == 0, p1, pltpu.roll(h, 1, 0))
    hm2 = jnp.where(row == 0, p2, jnp.where(row == 1, p1, pltpu.roll(h, 2, 0)))
    wcv = wconv_ref[...]
    conv = wcv[2:3, :] * h + wcv[1:2, :] * hm1 + wcv[0:1, :] * hm2
    g_ref[0] = (cb * conv).astype(BF16)
    carry_ref[...] = h[t - 8:t, :]

    qkv = _dot(u, wqkv_ref[...])
    scale = HEAD_DIM ** -0.5
    for hd in range(N_HEADS):
        lo = hd * HEAD_DIM
        q_ref[0, hd] = (qkv[:, lo:lo + HEAD_DIM] * scale).astype(BF16)
        k_ref[0, hd] = qkv[:, ATTN_W + lo:ATTN_W + lo + HEAD_DIM].astype(BF16)
        v_ref[0, hd] = qkv[:, 2 * ATTN_W + lo:2 * ATTN_W + lo + HEAD_DIM].astype(BF16)


def _inproj(x, mod, w_c, w_qkv, w_conv):
    b, s, d = x.shape
    cch = w_conv.shape[1]
    t = min(T_INPROJ, s)
    hshape = jax.ShapeDtypeStruct((b, N_HEADS, s, HEAD_DIM), BF16)
    hspec = pl.BlockSpec((1, N_HEADS, t, HEAD_DIM), lambda i, j: (i, 0, j, 0))
    return pl.pallas_call(
        _inproj_kernel,
        grid=(b, s // t),
        in_specs=[
            pl.BlockSpec((1, t, d), lambda i, j: (i, j, 0)),
            pl.BlockSpec((1, 6, d), lambda i, j: (i, 0, 0)),
            pl.BlockSpec(w_c.shape, lambda i, j: (0, 0)),
            pl.BlockSpec(w_qkv.shape, lambda i, j: (0, 0)),
            pl.BlockSpec(w_conv.shape, lambda i, j: (0, 0)),
        ],
        out_specs=[pl.BlockSpec((1, t, cch), lambda i, j: (i, j, 0)), hspec, hspec, hspec],
        out_shape=[jax.ShapeDtypeStruct((b, s, cch), BF16), hshape, hshape, hshape],
        scratch_shapes=[pltpu.VMEM((8, cch), F32)],
        compiler_params=_cparams(("arbitrary", "arbitrary")),
        name="inproj",
    )(x, mod, w_c, w_qkv, w_conv)


def _attn_kernel(q_ref, k_ref, v_ref, tri_ref, o_ref):
    qi = pl.program_id(2)
    q = q_ref[0, 0]
    tri = tri_ref[...]
    tq = q.shape[0]
    tk = tri.shape[0]

    def chunk(c, r, acc, mask):
        k0 = pl.multiple_of(c * tk, tk)
        kc = k_ref[0, 0, pl.ds(k0, tk), :]
        vc = v_ref[0, 0, pl.ds(k0, tk), :]
        z = _dot_nt(q, kc)
        lg = jnp.minimum(-z, 0.0) - jnp.log(1.0 + jnp.exp(-jnp.abs(z)))
        if mask is not None:
            lg = jnp.where(mask, lg, 0.0)
        hi, lo = _split_bf16(lg)
        cs = _dot(hi, tri) + _dot(lo, tri)
        p = jnp.exp(z + cs + r)
        if mask is not None:
            p = jnp.where(mask, p, 0.0)
        acc = acc + _dot(p.astype(BF16), vc)
        r = r + cs[:, 0:1]
        return r, acc

    rows = lax.broadcasted_iota(jnp.int32, (tq, tk), 0)
    cols = lax.broadcasted_iota(jnp.int32, (tq, tk), 1)
    r0 = jnp.zeros((tq, 1), F32)
    acc0 = jnp.zeros((tq, HEAD_DIM), F32)
    r, acc = chunk(qi, r0, acc0, cols < rows)

    def body(i, carry):
        return chunk(qi - 1 - i, carry[0], carry[1], None)

    r, acc = lax.fori_loop(0, qi, body, (r, acc))
    o_ref[0, 0] = acc.astype(BF16)


def _attention(q, k, v, tri):
    b, nh, s, hd = q.shape
    tq = min(TQ, s)
    assert tq == tri.shape[0]
    return pl.pallas_call(
        _attn_kernel,
        grid=(b, nh, s // tq),
        in_specs=[
            pl.BlockSpec((1, 1, tq, hd), lambda i, h, j: (i, h, j, 0)),
            pl.BlockSpec((1, 1, s, hd), lambda i, h, j: (i, h, 0, 0)),
            pl.BlockSpec((1, 1, s, hd), lambda i, h, j: (i, h, 0, 0)),
            pl.BlockSpec(tri.shape, lambda i, h, j: (0, 0)),
        ],
        out_specs=pl.BlockSpec((1, 1, tq, hd), lambda i, h, j: (i, h, j, 0)),
        out_shape=jax.ShapeDtypeStruct((b, nh, s, hd), BF16),
        compiler_params=_cparams(("arbitrary", "arbitrary", "arbitrary")),
        name="attn",
    )(q, k, v, tri)


def _layer_norm(h, g, b):
    mu = jnp.mean(h, axis=-1, keepdims=True)
    hc = h - mu
    var = jnp.mean(hc * hc, axis=-1, keepdims=True)
    return hc * lax.rsqrt(var + LN_EPS) * g + b


def _first_argmax(vals):
    best = vals[0]
    idx = jnp.zeros(best.shape, jnp.int32)
    for j in range(1, len(vals)):
        upd = vals[j] > best
        best = jnp.where(upd, vals[j], best)
        idx = jnp.where(upd, j, idx)
    return best, idx


def _pick(idx, vals):
    out = vals[0]
    for j in range(1, len(vals)):
        out = jnp.where(idx == j, vals[j], out)
    return out


def _post_kernel(alpha, x_ref, g_ref, o_ref, mod_ref, wg_ref, wa_ref, wb_ref, wo_ref, lng_ref, lnb_ref,
                 wrt_ref, br_ref, su_ref,
                 x1_ref, u2_ref, ri_ref, rw_ref, cnt_ref, carry_ref):
    first = jnp.logical_and(pl.program_id(0) == 0, pl.program_id(1) == 0)

    @pl.when(first)
    def _():
        carry_ref[...] = jnp.zeros_like(carry_ref)

    d = x_ref.shape[2]
    x = x_ref[0]
    sh1 = mod_ref[0, 0:1, :]
    sc1 = mod_ref[0, 1:2, :]
    g1 = mod_ref[0, 2:3, :]
    sh2 = mod_ref[0, 3:4, :]
    sc2 = mod_ref[0, 4:5, :]
    u = (x * (1.0 + sc1) + sh1).astype(BF16)
    gates = _dot(u, wg_ref[...])
    y_a = _dot(g_ref[0], wa_ref[...])
    y_b = _dot(o_ref[0, 0], wb_ref[0:HEAD_DIM, :])
    for hd in range(1, N_HEADS):
        y_b = y_b + _dot(o_ref[0, hd], wb_ref[hd * HEAD_DIM:(hd + 1) * HEAD_DIM, :])
    merged = jax.nn.sigmoid(gates[:, 0:d]) * y_a + jax.nn.sigmoid(gates[:, d:2 * d]) * y_b
    mix = _dot(merged.astype(BF16), wo_ref[...])
    x1 = _layer_norm(alpha * x + g1 * mix, lng_ref[...], lnb_ref[...])
    x1_ref[0] = x1
    u2 = x1 * (1.0 + sc2) + sh2
    u2_ref[0] = u2

    uh, ul = _split_bf16(u2)
    wh, wl = _split_bf16(wrt_ref[...])
    logits = _dot_nt(wh, uh) + _dot_nt(wh, ul) + _dot_nt(wl, uh)
    aff = jax.nn.sigmoid(logits)
    sel = aff + br_ref[...]
    selr = [sel[e:e + 1, :] for e in range(N_EXPERTS)]
    affr = [aff[e:e + 1, :] for e in range(N_EXPERTS)]
    gscore = []
    for gi in range(N_GROUPS):
        m = selr[gi * EXPERTS_PER_GROUP:(gi + 1) * EXPERTS_PER_GROUP]
        best = None
        for a in range(EXPERTS_PER_GROUP):
            for bb in range(a + 1, EXPERTS_PER_GROUP):
                pair = m[a] + m[bb]
                best = pair if best is None else jnp.maximum(best, pair)
        gscore.append(best)
    _, gidx = _first_argmax(gscore)
    sel_in = [_pick(gidx, [selr[gi * EXPERTS_PER_GROUP + j] for gi in range(N_GROUPS)]) for j in range(EXPERTS_PER_GROUP)]
    aff_in = [_pick(gidx, [affr[gi * EXPERTS_PER_GROUP + j] for gi in range(N_GROUPS)]) for j in range(EXPERTS_PER_GROUP)]
    _, i1 = _first_argmax(sel_in)
    neg = jnp.full_like(sel_in[0], -jnp.inf)
    _, i2 = _first_argmax([jnp.where(i1 == j, neg, sel_in[j]) for j in range(EXPERTS_PER_GROUP)])
    a1 = _pick(i1, aff_in)
    a2 = _pick(i2, aff_in)
    den = a1 + a2
    e1 = gidx * EXPERTS_PER_GROUP + i1
    e2 = gidx * EXPERTS_PER_GROUP + i2

    eio = lax.broadcasted_iota(jnp.int32, logits.shape, 0)
    hit1 = eio == e1
    hit2 = eio == e2
    member = jnp.where(jnp.logical_or(hit1, hit2), 1.0, 0.0)
    before = _dot(member.astype(BF16), su_ref[...]) + carry_ref[:, 0:1]
    rank1 = jnp.sum(jnp.where(hit1, before, 0.0), axis=0, keepdims=True)
    rank2 = jnp.sum(jnp.where(hit2, before, 0.0), axis=0, keepdims=True)
    t = member.shape[1]
    total = before[:, t - 1:t] + member[:, t - 1:t]
    carry_ref[...] = jnp.broadcast_to(total, carry_ref.shape)
    cnt_ref[...] = jnp.broadcast_to(total, cnt_ref.shape)

    zi = jnp.zeros_like(e1)
    ri_ref[0] = jnp.concatenate([e1, e2, rank1.astype(jnp.int32), rank2.astype(jnp.int32), zi, zi, zi, zi], axis=0)
    zf = jnp.zeros_like(a1)
    rw_ref[0] = jnp.concatenate([a1 / den, a2 / den, zf, zf, zf, zf, zf, zf], axis=0)


def _post(alpha, x, g, o, mod, w_g, w_a, w_b, w_o, ln_g, ln_b, w_rt, b_r, su):
    b, s, d = x.shape
    t = su.shape[0]
    nt = (b * s) // t
    tiles_per_b = s // t
    full2 = lambda a: pl.BlockSpec(a.shape, lambda i, j: (0, 0))
    tile_idx = lambda i, j: (i * tiles_per_b + j, 0, 0)
    return pl.pallas_call(
        functools.partial(_post_kernel, alpha),
        grid=(b, tiles_per_b),
        in_specs=[
            pl.BlockSpec((1, t, d), lambda i, j: (i, j, 0)),
            pl.BlockSpec((1, t, g.shape[2]), lambda i, j: (i, j, 0)),
            pl.BlockSpec((1, N_HEADS, t, HEAD_DIM), lambda i, j: (i, 0, j, 0)),
            pl.BlockSpec((1, 6, d), lambda i, j: (i, 0, 0)),
            full2(w_g), full2(w_a), full2(w_b), full2(w_o), full2(ln_g), full2(ln_b), full2(w_rt), full2(b_r), full2(su),
        ],
        out_specs=[
            pl.BlockSpec((1, t, d), lambda i, j: (i, j, 0)),
            pl.BlockSpec((1, t, d), lambda i, j: (i, j, 0)),
            pl.BlockSpec((1, 8, t), tile_idx),
            pl.BlockSpec((1, 8, t), tile_idx),
            pl.BlockSpec((N_EXPERTS, 128), lambda i, j: (0, 0)),
        ],
        out_shape=[
            jax.ShapeDtypeStruct((b, s, d), F32),
            jax.ShapeDtypeStruct((b, s, d), F32),
            jax.ShapeDtypeStruct((nt, 8, t), jnp.int32),
            jax.ShapeDtypeStruct((nt, 8, t), F32),
            jax.ShapeDtypeStruct((N_EXPERTS, 128), F32),
        ],
        scratch_shapes=[pltpu.VMEM((N_EXPERTS, 128), F32)],
        compiler_params=_cparams(("arbitrary", "arbitrary")),
        name="post",
    )(x, g, o, mod, w_g, w_a, w_b, w_o, ln_g, ln_b, w_rt, b_r, su)


def _row_copy(src_ref, src_row, dst_ref, dst_row, sem):
    return pltpu.make_async_copy(src_ref.at[pl.ds(src_row, 1)], dst_ref.at[pl.ds(dst_row, 1)], sem)


def _dispatch_kernel(dest_ref, u2_ref, xs_in_ref, xs_ref, sem):
    del xs_in_ref
    t = u2_ref.shape[0]

    def issue(i, c):
        _row_copy(u2_ref, i, xs_ref, dest_ref[0, 0, i], sem).start()
        _row_copy(u2_ref, i, xs_ref, dest_ref[0, 0, t + i], sem).start()
        return c

    lax.fori_loop(0, t, issue, 0)

    def drain(i, c):
        _row_copy(u2_ref, 0, xs_ref, 0, sem).wait()
        _row_copy(u2_ref, 0, xs_ref, 0, sem).wait()
        return c

    lax.fori_loop(0, t, drain, 0)


def _dispatch(dest, u2, xs_zero):
    n, d = u2.shape
    nt = dest.shape[0]
    t = n // nt
    return pl.pallas_call(
        _dispatch_kernel,
        grid=(nt,),
        in_specs=[
            pl.BlockSpec((1, 1, 2 * t), lambda i: (i, 0, 0), memory_space=pltpu.SMEM),
            pl.BlockSpec((t, d), lambda i: (i, 0)),
            pl.BlockSpec(memory_space=pl.ANY),
        ],
        out_specs=pl.BlockSpec(memory_space=pl.ANY),
        out_shape=jax.ShapeDtypeStruct(xs_zero.shape, xs_zero.dtype),
        scratch_shapes=[pltpu.SemaphoreType.DMA(())],
        input_output_aliases={2: 0},
        compiler_params=_cparams(("arbitrary",)),
        name="dispatch",
    )(dest, u2, xs_zero)


def _expert_kernel(be_ref, nu_ref, xs_ref, wg_ref, wu_ref, wd_ref, y_ref):
    i = pl.program_id(0)

    @pl.when(i < nu_ref[0])
    def _():
        xb = xs_ref[...].astype(BF16)
        hg = _dot(xb, wg_ref[0])
        hu = _dot(xb, wu_ref[0])
        h = (hg * jax.nn.sigmoid(hg) * hu).astype(BF16)
        y_ref[...] = _dot(h, wd_ref[0])

    @pl.when(i >= nu_ref[0])
    def _():
        y_ref[...] = jnp.zeros_like(y_ref)


def _experts(block_exp, n_used, xs, w_gate, w_up, w_down):
    p, d = xs.shape
    ff = w_gate.shape[2]
    nb = p // TM
    grid_spec = pltpu.PrefetchScalarGridSpec(
        num_scalar_prefetch=2,
        grid=(nb,),
        in_specs=[
            pl.BlockSpec((TM, d), lambda i, be, nu: (i, 0)),
            pl.BlockSpec((1, d, ff), lambda i, be, nu: (be[i], 0, 0)),
            pl.BlockSpec((1, d, ff), lambda i, be, nu: (be[i], 0, 0)),
            pl.BlockSpec((1, ff, d), lambda i, be, nu: (be[i], 0, 0)),
        ],
        out_specs=pl.BlockSpec((TM, d), lambda i, be, nu: (i, 0)),
    )
    return pl.pallas_call(
        _expert_kernel,
        grid_spec=grid_spec,
        out_shape=jax.ShapeDtypeStruct((p, d), F32),
        compiler_params=_cparams(("arbitrary",)),
        name="experts",
    )(block_exp, n_used, xs, w_gate, w_up, w_down)


def _combine_kernel(alpha, dest_ref, w_ref, x1_ref, mod_ref, lng_ref, lnb_ref, y_ref, o_ref, ybuf, sem):
    t = x1_ref.shape[0]

    def issue(i, c):
        _row_copy(y_ref, dest_ref[0, 0, i], ybuf.at[0], i, sem).start()
        _row_copy(y_ref, dest_ref[0, 0, t + i], ybuf.at[1], i, sem).start()
        return c

    lax.fori_loop(0, t, issue, 0)

    def drain(i, c):
        _row_copy(y_ref, 0, ybuf.at[0], 0, sem).wait()
        _row_copy(y_ref, 0, ybuf.at[1], 0, sem).wait()
        return c

    lax.fori_loop(0, t, drain, 0)

    g2 = mod_ref[0, 5:6, :]
    ffn = w_ref[:, 0:1] * ybuf[0] + w_ref[:, 1:2] * ybuf[1]
    o_ref[...] = _layer_norm(alpha * x1_ref[...] + g2 * ffn, lng_ref[...], lnb_ref[...])


def _combine(alpha, dest, w, x1, mod, ln_g, ln_b, y, tiles_per_b):
    n, d = x1.shape
    nt = dest.shape[0]
    t = n // nt
    return pl.pallas_call(
        functools.partial(_combine_kernel, alpha),
        grid=(nt,),
        in_specs=[
            pl.BlockSpec((1, 1, 2 * t), lambda i: (i, 0, 0), memory_space=pltpu.SMEM),
            pl.BlockSpec((t, 2), lambda i: (i, 0)),
            pl.BlockSpec((t, d), lambda i: (i, 0)),
            pl.BlockSpec((1, 6, d), lambda i: (i // tiles_per_b, 0, 0)),
            pl.BlockSpec(ln_g.shape, lambda i: (0, 0)),
            pl.BlockSpec(ln_b.shape, lambda i: (0, 0)),
            pl.BlockSpec(memory_space=pl.ANY),
        ],
        out_specs=pl.BlockSpec((t, d), lambda i: (i, 0)),
        out_shape=jax.ShapeDtypeStruct((n, d), F32),
        scratch_shapes=[pltpu.VMEM((2, t, d), F32), pltpu.SemaphoreType.DMA(())],
        compiler_params=_cparams(("arbitrary",)),
        name="combine",
    )(dest, w, x1, mod, ln_g, ln_b, y)


def kernel(x, c, w_ada, b_ada, w_in, w_conv, w_a, w_b, w_o, ln1_g, ln1_b, w_router, b_router, w_gate, w_up, w_down,
           ln2_g, ln2_b):
    bsz, seq, d = x.shape
    depth = w_ada.shape[0]
    n = bsz * seq
    cch = w_conv.shape[2]
    alpha = float((2 * depth) ** 0.25)
    n_conv = 3 * cch
    n_qkv = 3 * ATTN_W

    c_pad = jnp.zeros((8, d), F32).at[:bsz].set(c)
    mod_all = _ada(c_pad, w_ada, b_ada)

    tq = min(TQ, seq)
    ii = jnp.arange(tq, dtype=jnp.int32)
    tri = (ii[:, None] >= ii[None, :]).astype(BF16)
    t_post = min(T_POST, seq)
    jj = jnp.arange(t_post, dtype=jnp.int32)
    su = (jj[:, None] < jj[None, :]).astype(BF16)
    w_rt = w_router.T
    b_r = b_router.reshape(N_EXPERTS, 1)

    a_rows = n * TOP_K
    n_blocks = -(-a_rows // TM) + N_EXPERTS
    p_rows = n_blocks * TM
    tiles_per_b = seq // t_post

    for l in range(depth):
        mod = mod_all[l, :bsz].reshape(bsz, 6, d)
        w_in_l = w_in[l].astype(BF16)
        g, q, k, v = _inproj(x, mod, w_in_l[:, :n_conv], w_in_l[:, n_conv:n_conv + n_qkv], w_conv[l])
        o = _attention(q, k, v, tri)
        x1, u2, ri, rw, cnt = _post(
            alpha, x, g, o, mod, w_in_l[:, n_conv + n_qkv:], w_a[l].astype(BF16), w_b[l].astype(BF16),
            w_o[l].astype(BF16), ln1_g[l].reshape(1, d), ln1_b[l].reshape(1, d), w_rt, b_r, su)

        counts = cnt[:, 0].astype(jnp.int32)
        pad_counts = (counts + TM - 1) // TM * TM
        pad_end = jnp.cumsum(pad_counts)
        pad_start = pad_end - pad_counts
        e12 = ri[:, 0:2, :]
        dest = (pad_start[e12] + ri[:, 2:4, :]).reshape(-1, 1, 2 * t_post)
        block_start = jnp.arange(n_blocks, dtype=jnp.int32) * TM
        block_exp = jnp.clip(jnp.searchsorted(pad_end, block_start, side='right'), 0, N_EXPERTS - 1).astype(jnp.int32)
        n_used = (pad_end[N_EXPERTS - 1] // TM).astype(jnp.int32).reshape(1)
        gate_w = jnp.transpose(rw[:, 0:2, :], (0, 2, 1)).reshape(n, 2)

        xs = _dispatch(dest, u2.reshape(n, d), jnp.zeros((p_rows, d), F32))
        y = _experts(block_exp, n_used, xs, w_gate[l].astype(BF16), w_up[l].astype(BF16), w_down[l].astype(BF16))
        x = _combine(alpha, dest, gate_w, x1.reshape(n, d), mod, ln2_g[l].reshape(1, d), ln2_b[l].reshape(1, d), y,
                     tiles_per_b).reshape(bsz, seq, d)
    return x
```

```python
import functools

import jax
import jax.numpy as jnp
from jax import lax
from jax.experimental import pallas as pl
from jax.experimental.pallas import tpu as pltpu

F32 = jnp.float32
BF16 = jnp.bfloat16

N_HEADS = 8
HEAD_DIM = 64
ATTN_W = N_HEADS * HEAD_DIM
N_EXPERTS = 16
N_GROUPS = 4
EXPERTS_PER_GROUP = N_EXPERTS // N_GROUPS
TOP_K = 2
CONV_K = 3
LN_EPS = 1e-5

T_INPROJ = 512
T_POST = 256
TK = 256
ATTN_HEADS_PER_STEP = 4
N_NEAR = 3
TM = 256
DMA_UNROLL = 8
ADA_TN = 1536

R_STOP = 110.0

VMEM_LIMIT = 56 * 1024 * 1024
ATTN_FLAGS = None


def _cparams(sem, flags=None):
    return pltpu.CompilerParams(dimension_semantics=sem, vmem_limit_bytes=VMEM_LIMIT, flags=flags)


def _dot(a, b):
    return jnp.dot(a, b, preferred_element_type=F32)


def _dot_nt(a, b):
    return lax.dot_general(a, b, (((1,), (1,)), ((), ())), preferred_element_type=F32)


def _split_bf16(v):
    hi = v.astype(BF16)
    lo = (v - hi.astype(F32)).astype(BF16)
    return hi, lo


def _ada_kernel(c_ref, w_ref, b_ref, o_ref):
    c = c_ref[...]
    s = c * jax.nn.sigmoid(c)
    o_ref[0] = jnp.dot(s, w_ref[0], precision=lax.Precision.HIGHEST, preferred_element_type=F32) + b_ref[0]


def _ada(c_pad, w_ada, b_ada):
    depth, d, n6 = w_ada.shape
    rows = c_pad.shape[0]
    return pl.pallas_call(
        _ada_kernel,
        grid=(depth, n6 // ADA_TN),
        in_specs=[
            pl.BlockSpec((rows, d), lambda l, j: (0, 0)),
            pl.BlockSpec((1, d, ADA_TN), lambda l, j: (l, 0, j)),
            pl.BlockSpec((1, 1, ADA_TN), lambda l, j: (l, 0, j)),
        ],
        out_specs=pl.BlockSpec((1, rows, ADA_TN), lambda l, j: (l, 0, j)),
        out_shape=jax.ShapeDtypeStruct((depth, rows, n6), F32),
        compiler_params=_cparams(("arbitrary", "arbitrary")),
        name="adaln",
    )(c_pad, w_ada, b_ada.reshape(depth, 1, n6))


def _inproj_kernel(x_ref, mod_ref, wc_ref, wqk_ref, wvt_ref, wconv_ref, g_ref, q_ref, k_ref, vt_ref, carry_ref):
    j = pl.program_id(1)
    t = x_ref.shape[1]
    cch = wconv_ref.shape[1]
    tk = vt_ref.shape[4]

    @pl.when(j == 0)
    def _():
        carry_ref[...] = jnp.zeros_like(carry_ref)

    sh1 = mod_ref[0, 0:1, :]
    sc1 = mod_ref[0, 1:2, :]
    u = (x_ref[0] * (1.0 + sc1) + sh1).astype(BF16)

    pc = _dot(u, wc_ref[...])
    cb = pc[:, 0:cch]
    h = pc[:, cch:2 * cch] * pc[:, 2 * cch:3 * cch]
    prev = carry_ref[...]
    p1 = prev[7:8, :]
    p2 = prev[6:7, :]
    row = lax.broadcasted_iota(jnp.int32, h.shape, 0)
    hm1 = jnp.where(row == 0, p1, pltpu.roll(h, 1, 0))
    hm2 = jnp.where(row == 0, p2, jnp.where(row == 1, p1, pltpu.roll(h, 2, 0)))
    wcv = wconv_ref[...]
    conv = wcv[2:3, :] * h + wcv[1:2, :] * hm1 + wcv[0:1, :] * hm2
    g_ref[0] = (cb * conv).astype(BF16)
    carry_ref[...] = h[t - 8:t, :]

    qk = _dot(u, wqk_ref[...])
    scale = HEAD_DIM ** -0.5
    for hd in range(N_HEADS):
        lo = hd * HEAD_DIM
        q_ref[0, hd] = (qk[:, lo:lo + HEAD_DIM] * scale).astype(BF16)
        k_ref[0, hd] = qk[:, ATTN_W + lo:ATTN_W + lo + HEAD_DIM].astype(BF16)
    vt = _dot_nt(wvt_ref[...], u)
    for hd in range(N_HEADS):
        for cj in range(t // tk):
            vt_ref[0, hd, cj] = vt[hd * HEAD_DIM:(hd + 1) * HEAD_DIM, cj * tk:(cj + 1) * tk].astype(BF16)


def _inproj(x, mod, w_c, w_qk, w_vt, w_conv, tk):
    b, s, d = x.shape
    cch = w_conv.shape[1]
    t = min(T_INPROJ, s)
    hshape = jax.ShapeDtypeStruct((b, N_HEADS, s, HEAD_DIM), BF16)
    hspec = pl.BlockSpec((1, N_HEADS, t, HEAD_DIM), lambda i, j: (i, 0, j, 0))
    return pl.pallas_call(
        _inproj_kernel,
        grid=(b, s // t),
        in_specs=[
            pl.BlockSpec((1, t, d), lambda i, j: (i, j, 0)),
            pl.BlockSpec((1, 6, d), lambda i, j: (i, 0, 0)),
            pl.BlockSpec(w_c.shape, lambda i, j: (0, 0)),
            pl.BlockSpec(w_qk.shape, lambda i, j: (0, 0)),
            pl.BlockSpec(w_vt.shape, lambda i, j: (0, 0)),
            pl.BlockSpec(w_conv.shape, lambda i, j: (0, 0)),
        ],
        out_specs=[
            pl.BlockSpec((1, t, cch), lambda i, j: (i, j, 0)), hspec, hspec,
            pl.BlockSpec((1, N_HEADS, t // tk, HEAD_DIM, tk), lambda i, j: (i, 0, j, 0, 0)),
        ],
        out_shape=[
            jax.ShapeDtypeStruct((b, s, cch), BF16), hshape, hshape,
            jax.ShapeDtypeStruct((b, N_HEADS, s // tk, HEAD_DIM, tk), BF16),
        ],
        scratch_shapes=[pltpu.VMEM((8, cch), F32)],
        compiler_params=_cparams(("arbitrary", "arbitrary")),
        name="inproj",
    )(x, mod, w_c, w_qk, w_vt, w_conv)


def _neg_abs(z):
    bits = lax.bitcast_convert_type(z, jnp.uint32) | jnp.uint32(0x80000000)
    return lax.bitcast_convert_type(bits, F32)


def _attn_kernel(q_ref, k_ref, vt_ref, trit_ref, ot_ref, za_buf, zb_buf, h_buf, r_buf, acc_buf, sat_ref):
    qi = pl.program_id(2)
    tk = za_buf.shape[2]
    nhead = za_buf.shape[1]
    rows = lax.broadcasted_iota(jnp.int32, (tk, tk), 0)
    cols = lax.broadcasted_iota(jnp.int32, (tk, tk), 1)
    causal = rows < cols

    chunk = lambda dist: jnp.maximum(qi - dist, 0)
    all_heads = tuple(range(nhead))

    def step(sc, sp, sw):
        cs = {}
        if sw is not None:
            dw, heads_w, slotw = sw
            for s in heads_w:
                cs[s] = _dot(trit_ref[...], h_buf[slotw, s])
        if sc is not None:
            dc, heads_c, slotc = sc
            k0 = pl.multiple_of(chunk(dc) * tk, tk)
            for s in heads_c:
                za_buf[slotc, s] = _dot_nt(k_ref[0, s, pl.ds(k0, tk), :], q_ref[0, s])
        if sp is not None:
            dp, heads_p, slotp = sp
            for s in heads_p:
                zt = za_buf[slotp, s]
                v = jnp.maximum(zt, 0.0) + jnp.log(1.0 + jnp.exp(_neg_abs(zt)))
                zb_buf[slotp, s] = zt - v
                if isinstance(dp, int) and dp == 0:
                    v = jnp.where(causal, v, 0.0)
                h_buf[slotp, s] = v.astype(BF16)
        if sw is not None:
            cw = chunk(dw)
            on = None if isinstance(dw, int) and dw == 0 else (dw <= qi).astype(F32)
            for s in heads_w:
                r = r_buf[s]
                p = jnp.exp(zb_buf[slotw, s] - cs[s])
                if on is None:
                    p = jnp.where(causal, p, 0.0)
                scale = jnp.exp(-r)
                add = cs[s][0:1, :] + h_buf[slotw, s, 0:1, :].astype(F32)
                if on is not None:
                    scale = scale * on
                    add = add * on
                acc_buf[s] += _dot(vt_ref[0, s, cw], p.astype(BF16)) * scale
                r_buf[s] = r + add

    def saturated():
        return (jnp.min(r_buf[...]) >= R_STOP).astype(jnp.int32)

    r_buf[...] = jnp.zeros_like(r_buf)
    acc_buf[...] = jnp.zeros_like(acc_buf)

    items = [(d, (s,), d) for d in range(N_NEAR) for s in range(nhead)]
    at = lambda m: items[m] if 0 <= m < len(items) else None
    for m in range(len(items) + 2):
        step(at(m), at(m - 1), at(m - 2))

    sat_ref[0] = saturated()

    @pl.when(jnp.logical_and(qi >= N_NEAR, sat_ref[0] == 0))
    def _():
        p0, p1 = N_NEAR % 2, (N_NEAR + 1) % 2
        step((N_NEAR, all_heads, p0), None, None)
        step((N_NEAR + 1, all_heads, p1), (N_NEAR, all_heads, p0), None)

        def not_done(carry):
            j, done = carry
            return jnp.logical_and(N_NEAR + 2 * j <= qi, done == 0)

        def pair(carry):
            j, _ = carry
            d = N_NEAR + 2 * j
            step((d + 2, all_heads, p0), (d + 1, all_heads, p1), (d, all_heads, p0))
            sat_ref[0] = saturated()

            @pl.when(jnp.logical_and(d + 1 <= qi, sat_ref[0] == 0))
            def _():
                step((d + 3, all_heads, p1), (d + 2, all_heads, p0), (d + 1, all_heads, p1))
                sat_ref[0] = saturated()

            return j + 1, sat_ref[0]

        lax.while_loop(not_done, pair, (jnp.int32(0), jnp.int32(0)))

    for s in range(nhead):
        ot_ref[0, s] = acc_buf[s].astype(BF16)


def _attention(q, k, vt, trit):
    b, nh, s, hd = q.shape
    tk = trit.shape[0]
    g = ATTN_HEADS_PER_STEP
    return pl.pallas_call(
        _attn_kernel,
        grid=(b, nh // g, s // tk),
        in_specs=[
            pl.BlockSpec((1, g, tk, hd), lambda i, h, j: (i, h, j, 0)),
            pl.BlockSpec((1, g, s, hd), lambda i, h, j: (i, h, 0, 0)),
            pl.BlockSpec((1, g, s // tk, hd, tk), lambda i, h, j: (i, h, 0, 0, 0)),
            pl.BlockSpec(trit.shape, lambda i, h, j: (0, 0)),
        ],
        out_specs=pl.BlockSpec((1, g, hd, tk), lambda i, h, j: (i, h, 0, j)),
        out_shape=jax.ShapeDtypeStruct((b, nh, hd, s), BF16),
        scratch_shapes=[
            pltpu.VMEM((N_NEAR, g, tk, tk), F32),
            pltpu.VMEM((N_NEAR, g, tk, tk), F32),
            pltpu.VMEM((N_NEAR, g, tk, tk), BF16),
            pltpu.VMEM((g, 1, tk), F32),
            pltpu.VMEM((g, hd, tk), F32),
            pltpu.SMEM((1,), jnp.int32),
        ],
        compiler_params=_cparams(("arbitrary", "arbitrary", "arbitrary"), ATTN_FLAGS),
        name="attn",
    )(q, k, vt, trit)


def _layer_norm(h, g, b):
    mu = jnp.mean(h, axis=-1, keepdims=True)
    hc = h - mu
    var = jnp.mean(hc * hc, axis=-1, keepdims=True)
    return hc * lax.rsqrt(var + LN_EPS) * g + b


def _first_argmax(vals):
    best = vals[0]
    idx = jnp.zeros(best.shape, jnp.int32)
    for j in range(1, len(vals)):
        upd = vals[j] > best
        best = jnp.where(upd, vals[j], best)
        idx = jnp.where(upd, j, idx)
    return best, idx


def _pick(idx, vals):
    out = vals[0]
    for j in range(1, len(vals)):
        out = jnp.where(idx == j, vals[j], out)
    return out


def _post_kernel(alpha, x_ref, g_ref, o_ref, mod_ref, wg_ref, wa_ref, wb_ref, wo_ref, lng_ref, lnb_ref,
                 wrt_ref, br_ref, su_ref,
                 x1_ref, u2_ref, ri_ref, rw_ref, cnt_ref, carry_ref):
    first = jnp.logical_and(pl.program_id(0) == 0, pl.program_id(1) == 0)

    @pl.when(first)
    def _():
        carry_ref[...] = jnp.zeros_like(carry_ref)

    d = x_ref.shape[2]
    x = x_ref[0]
    sh1 = mod_ref[0, 0:1, :]
    sc1 = mod_ref[0, 1:2, :]
    g1 = mod_ref[0, 2:3, :]
    sh2 = mod_ref[0, 3:4, :]
    sc2 = mod_ref[0, 4:5, :]
    u = (x * (1.0 + sc1) + sh1).astype(BF16)
    gates = _dot(u, wg_ref[...])
    y_a = _dot(g_ref[0], wa_ref[...])
    o_t = o_ref[0].reshape(ATTN_W, x.shape[0])
    y_b = lax.dot_general(o_t, wb_ref[...], (((0,), (0,)), ((), ())), preferred_element_type=F32)
    merged = jax.nn.sigmoid(gates[:, 0:d]) * y_a + jax.nn.sigmoid(gates[:, d:2 * d]) * y_b
    mix = _dot(merged.astype(BF16), wo_ref[...])
    x1 = _layer_norm(alpha * x + g1 * mix, lng_ref[...], lnb_ref[...])
    x1_ref[0] = x1
    u2 = x1 * (1.0 + sc2) + sh2
    u2_ref[0] = u2

    uh, ul = _split_bf16(u2)
    wh, wl = _split_bf16(wrt_ref[...])
    logits = _dot_nt(wh, uh) + _dot_nt(wh, ul) + _dot_nt(wl, uh)
    aff = jax.nn.sigmoid(logits)
    sel = aff + br_ref[...]
    selr = [sel[e:e + 1, :] for e in range(N_EXPERTS)]
    affr = [aff[e:e + 1, :] for e in range(N_EXPERTS)]
    gscore = []
    for gi in range(N_GROUPS):
        m = selr[gi * EXPERTS_PER_GROUP:(gi + 1) * EXPERTS_PER_GROUP]
        best = None
        for a in range(EXPERTS_PER_GROUP):
            for bb in range(a + 1, EXPERTS_PER_GROUP):
                pair = m[a] + m[bb]
                best = pair if best is None else jnp.maximum(best, pair)
        gscore.append(best)
    _, gidx = _first_argmax(gscore)
    sel_in = [_pick(gidx, [selr[gi * EXPERTS_PER_GROUP + j] for gi in range(N_GROUPS)]) for j in range(EXPERTS_PER_GROUP)]
    aff_in = [_pick(gidx, [affr[gi * EXPERTS_PER_GROUP + j] for gi in range(N_GROUPS)]) for j in range(EXPERTS_PER_GROUP)]
    _, i1 = _first_argmax(sel_in)
    neg = jnp.full_like(sel_in[0], -jnp.inf)
    _, i2 = _first_argmax([jnp.where(i1 == j, neg, sel_in[j]) for j in range(EXPERTS_PER_GROUP)])
    a1 = _pick(i1, aff_in)
    a2 = _pick(i2, aff_in)
    den = a1 + a2
    e1 = gidx * EXPERTS_PER_GROUP + i1
    e2 = gidx * EXPERTS_PER_GROUP + i2

    eio = lax.broadcasted_iota(jnp.int32, logits.shape, 0)
    hit1 = eio == e1
    hit2 = eio == e2
    member = jnp.where(jnp.logical_or(hit1, hit2), 1.0, 0.0)
    before = _dot(member.astype(BF16), su_ref[...]) + carry_ref[:, 0:1]
    rank1 = jnp.sum(jnp.where(hit1, before, 0.0), axis=0, keepdims=True)
    rank2 = jnp.sum(jnp.where(hit2, before, 0.0), axis=0, keepdims=True)
    t = member.shape[1]
    total = before[:, t - 1:t] + member[:, t - 1:t]
    carry_ref[...] = jnp.broadcast_to(total, carry_ref.shape)
    cnt_ref[...] = jnp.broadcast_to(total, cnt_ref.shape)

    zi = jnp.zeros_like(e1)
    ri_ref[0] = jnp.concatenate([e1, e2, rank1.astype(jnp.int32), rank2.astype(jnp.int32), zi, zi, zi, zi], axis=0)
    zf = jnp.zeros_like(a1)
    rw_ref[0] = jnp.concatenate([a1 / den, a2 / den, zf, zf, zf, zf, zf, zf], axis=0)


def _post(alpha, x, g, o, mod, w_g, w_a, w_b, w_o, ln_g, ln_b, w_rt, b_r, su):
    b, s, d = x.shape
    t = su.shape[0]
    nt = (b * s) // t
    tiles_per_b = s // t
    full2 = lambda a: pl.BlockSpec(a.shape, lambda i, j: (0, 0))
    tile_idx = lambda i, j: (i * tiles_per_b + j, 0, 0)
    return pl.pallas_call(
        functools.partial(_post_kernel, alpha),
        grid=(b, tiles_per_b),
        in_specs=[
            pl.BlockSpec((1, t, d), lambda i, j: (i, j, 0)),
            pl.BlockSpec((1, t, g.shape[2]), lambda i, j: (i, j, 0)),
            pl.BlockSpec((1, N_HEADS, HEAD_DIM, t), lambda i, j: (i, 0, 0, j)),
            pl.BlockSpec((1, 6, d), lambda i, j: (i, 0, 0)),
            full2(w_g), full2(w_a), full2(w_b), full2(w_o), full2(ln_g), full2(ln_b), full2(w_rt), full2(b_r), full2(su),
        ],
        out_specs=[
            pl.BlockSpec((1, t, d), lambda i, j: (i, j, 0)),
            pl.BlockSpec((1, t, d), lambda i, j: (i, j, 0)),
            pl.BlockSpec((1, 8, t), tile_idx),
            pl.BlockSpec((1, 8, t), tile_idx),
            pl.BlockSpec((N_EXPERTS, 128), lambda i, j: (0, 0)),
        ],
        out_shape=[
            jax.ShapeDtypeStruct((b, s, d), F32),
            jax.ShapeDtypeStruct((b, s, d), F32),
            jax.ShapeDtypeStruct((nt, 8, t), jnp.int32),
            jax.ShapeDtypeStruct((nt, 8, t), F32),
            jax.ShapeDtypeStruct((N_EXPERTS, 128), F32),
        ],
        scratch_shapes=[pltpu.VMEM((N_EXPERTS, 128), F32)],
        compiler_params=_cparams(("arbitrary", "arbitrary")),
        name="post",
    )(x, g, o, mod, w_g, w_a, w_b, w_o, ln_g, ln_b, w_rt, b_r, su)


def _row_copy(src_ref, src_row, dst_ref, dst_row, sem):
    return pltpu.make_async_copy(src_ref.at[pl.ds(src_row, 1)], dst_ref.at[pl.ds(dst_row, 1)], sem)


def _dispatch_kernel(last_ref, used_ref, nu_ref, dest_ref, u2_ref, xs_ref, zero_buf, sem, zsem):
    t = u2_ref.shape[0]
    tm = zero_buf.shape[0]

    @pl.when(pl.program_id(0) == 0)
    def _():
        zero_buf[...] = jnp.zeros_like(zero_buf)

        def zero_block(row0):
            cp = pltpu.make_async_copy(zero_buf, xs_ref.at[pl.ds(pl.multiple_of(row0, tm), tm)], zsem)
            cp.start()
            cp.wait()

        for e in range(N_EXPERTS):
            @pl.when(used_ref[e] > 0)
            def _():
                zero_block(last_ref[e])

        def tail(b, c):
            zero_block(b * tm)
            return c

        lax.fori_loop(nu_ref[0], xs_ref.shape[0] // tm, tail, 0)

    def issue(g, c):
        for u in range(DMA_UNROLL):
            i = g * DMA_UNROLL + u
            _row_copy(u2_ref, i, xs_ref, dest_ref[0, 0, i], sem).start(priority=u % 2)
            _row_copy(u2_ref, i, xs_ref, dest_ref[0, 0, t + i], sem).start(priority=(u + 1) % 2)
        return c

    lax.fori_loop(0, t // DMA_UNROLL, issue, 0)
    for _ in range(TOP_K):
        pltpu.make_async_copy(u2_ref, xs_ref.at[pl.ds(0, t)], sem).wait()


def _dispatch(last_block_row, seg_used, n_used, dest, u2, p_rows):
    n, d = u2.shape
    nt = dest.shape[0]
    t = n // nt
    grid_spec = pltpu.PrefetchScalarGridSpec(
        num_scalar_prefetch=3,
        grid=(nt,),
        in_specs=[
            pl.BlockSpec((1, 1, 2 * t), lambda i, lb, us, nu: (i, 0, 0), memory_space=pltpu.SMEM),
            pl.BlockSpec((t, d), lambda i, lb, us, nu: (i, 0)),
        ],
        out_specs=pl.BlockSpec(memory_space=pl.ANY),
        scratch_shapes=[pltpu.VMEM((TM, d), F32), pltpu.SemaphoreType.DMA(()), pltpu.SemaphoreType.DMA(())],
    )
    return pl.pallas_call(
        _dispatch_kernel,
        grid_spec=grid_spec,
        out_shape=jax.ShapeDtypeStruct((p_rows, d), F32),
        compiler_params=_cparams(("arbitrary",)),
        name="dispatch",
    )(last_block_row, seg_used, n_used, dest, u2)


def _expert_kernel(be_ref, nu_ref, xs_ref, wg_ref, wu_ref, wd_ref, y_ref):
    i = pl.program_id(0)

    @pl.when(i < nu_ref[0])
    def _():
        xb = xs_ref[...].astype(BF16)
        hg = _dot(xb, wg_ref[0])
        hu = _dot(xb, wu_ref[0])
        h = (hg * jax.nn.sigmoid(hg) * hu).astype(BF16)
        y_ref[...] = _dot(h, wd_ref[0])

    @pl.when(i >= nu_ref[0])
    def _():
        y_ref[...] = jnp.zeros_like(y_ref)


def _experts(block_exp, n_used, xs, w_gate, w_up, w_down):
    p, d = xs.shape
    ff = w_gate.shape[2]
    nb = p // TM
    grid_spec = pltpu.PrefetchScalarGridSpec(
        num_scalar_prefetch=2,
        grid=(nb,),
        in_specs=[
            pl.BlockSpec((TM, d), lambda i, be, nu: (jnp.minimum(i, nu[0] - 1), 0)),
            pl.BlockSpec((1, d, ff), lambda i, be, nu: (be[i], 0, 0)),
            pl.BlockSpec((1, d, ff), lambda i, be, nu: (be[i], 0, 0)),
            pl.BlockSpec((1, ff, d), lambda i, be, nu: (be[i], 0, 0)),
        ],
        out_specs=pl.BlockSpec((TM, d), lambda i, be, nu: (i, 0)),
    )
    return pl.pallas_call(
        _expert_kernel,
        grid_spec=grid_spec,
        out_shape=jax.ShapeDtypeStruct((p, d), F32),
        compiler_params=_cparams(("arbitrary",)),
        name="experts",
    )(block_exp, n_used, xs, w_gate, w_up, w_down)


def _combine_kernel(alpha, dest_ref, dnext_ref, w_ref, x1_ref, mod_ref, lng_ref, lnb_ref, y_ref, o_ref, ybuf, sems):
    i = pl.program_id(0)
    t = x1_ref.shape[0]
    slot = lax.rem(i, 2)

    def gather(dref, sl):
        def issue(g, c):
            for u in range(DMA_UNROLL):
                r = g * DMA_UNROLL + u
                _row_copy(y_ref, dref[0, 0, r], ybuf.at[sl, 0], r, sems.at[sl]).start(priority=u % 2)
                _row_copy(y_ref, dref[0, 0, t + r], ybuf.at[sl, 1], r, sems.at[sl]).start(priority=(u + 1) % 2)
            return c

        lax.fori_loop(0, t // DMA_UNROLL, issue, 0)

    @pl.when(i == 0)
    def _():
        gather(dest_ref, 0)

    @pl.when(i + 1 < pl.num_programs(0))
    def _():
        gather(dnext_ref, 1 - slot)

    for k in range(TOP_K):
        pltpu.make_async_copy(y_ref.at[pl.ds(0, t)], ybuf.at[slot, k], sems.at[slot]).wait()

    g2 = mod_ref[0, 5:6, :]
    ffn = w_ref[:, 0:1] * ybuf[slot, 0] + w_ref[:, 1:2] * ybuf[slot, 1]
    o_ref[...] = _layer_norm(alpha * x1_ref[...] + g2 * ffn, lng_ref[...], lnb_ref[...])


def _combine(alpha, dest, w, x1, mod, ln_g, ln_b, y, tiles_per_b):
    n, d = x1.shape
    nt = dest.shape[0]
    t = n // nt
    return pl.pallas_call(
        functools.partial(_combine_kernel, alpha),
        grid=(nt,),
        in_specs=[
            pl.BlockSpec((1, 1, 2 * t), lambda i: (i, 0, 0), memory_space=pltpu.SMEM),
            pl.BlockSpec((1, 1, 2 * t), lambda i: (jnp.minimum(i + 1, nt - 1), 0, 0), memory_space=pltpu.SMEM),
            pl.BlockSpec((t, 2), lambda i: (i, 0)),
            pl.BlockSpec((t, d), lambda i: (i, 0)),
            pl.BlockSpec((1, 6, d), lambda i: (i // tiles_per_b, 0, 0)),
            pl.BlockSpec(ln_g.shape, lambda i: (0, 0)),
            pl.BlockSpec(ln_b.shape, lambda i: (0, 0)),
            pl.BlockSpec(memory_space=pl.ANY),
        ],
        out_specs=pl.BlockSpec((t, d), lambda i: (i, 0)),
        out_shape=jax.ShapeDtypeStruct((n, d), F32),
        scratch_shapes=[pltpu.VMEM((2, TOP_K, t, d), F32), pltpu.SemaphoreType.DMA((2,))],
        compiler_params=_cparams(("arbitrary",)),
        name="combine",
    )(dest, dest, w, x1, mod, ln_g, ln_b, y)


def kernel(x, c, w_ada, b_ada, w_in, w_conv, w_a, w_b, w_o, ln1_g, ln1_b, w_router, b_router, w_gate, w_up, w_down,
           ln2_g, ln2_b):
    bsz, seq, d = x.shape
    depth = w_ada.shape[0]
    n = bsz * seq
    cch = w_conv.shape[2]
    alpha = float((2 * depth) ** 0.25)
    n_conv = 3 * cch
    n_qkv = 3 * ATTN_W

    c_pad = jnp.zeros((8, d), F32).at[:bsz].set(c)
    mod_all = _ada(c_pad, w_ada, b_ada)

    tk = min(TK, seq)
    ii = jnp.arange(tk, dtype=jnp.int32)
    trit = (ii[None, :] > ii[:, None]).astype(BF16)
    t_post = min(T_POST, seq)
    jj = jnp.arange(t_post, dtype=jnp.int32)
    su = (jj[:, None] < jj[None, :]).astype(BF16)
    w_rt = w_router.T
    b_r = b_router.reshape(N_EXPERTS, 1)

    a_rows = n * TOP_K
    n_blocks = -(-a_rows // TM) + N_EXPERTS
    p_rows = n_blocks * TM
    tiles_per_b = seq // t_post

    for l in range(depth):
        mod = mod_all[l, :bsz].reshape(bsz, 6, d)
        w_in_l = w_in[l].astype(BF16)
        n_qk = 2 * ATTN_W
        g, q, k, vt = _inproj(x, mod, w_in_l[:, :n_conv], w_in_l[:, n_conv:n_conv + n_qk],
                              w_in_l[:, n_conv + n_qk:n_conv + n_qkv].T, w_conv[l], tk)
        o = _attention(q, k, vt, trit)
        x1, u2, ri, rw, cnt = _post(
            alpha, x, g, o, mod, w_in_l[:, n_conv + n_qkv:], w_a[l].astype(BF16), w_b[l].astype(BF16),
            w_o[l].astype(BF16), ln1_g[l].reshape(1, d), ln1_b[l].reshape(1, d), w_rt, b_r, su)

        counts = cnt[:, 0].astype(jnp.int32)
        pad_counts = (counts + TM - 1) // TM * TM
        pad_end = jnp.cumsum(pad_counts)
        pad_start = pad_end - pad_counts
        e12 = ri[:, 0:2, :]
        e_ids = jnp.arange(N_EXPERTS, dtype=jnp.int32)
        seg_start = jnp.sum(jnp.where(e12[..., None] == e_ids, pad_start, 0), axis=-1)
        dest = (seg_start + ri[:, 2:4, :]).reshape(-1, 1, 2 * t_post)
        block_start = jnp.arange(n_blocks, dtype=jnp.int32) * TM
        block_exp = jnp.minimum(jnp.sum((block_start[:, None] >= pad_end[None, :]).astype(jnp.int32), axis=1),
                                N_EXPERTS - 1)
        n_used = (pad_end[N_EXPERTS - 1] // TM).astype(jnp.int32).reshape(1)
        gate_w = jnp.transpose(rw[:, 0:2, :], (0, 2, 1)).reshape(n, 2)

        xs = _dispatch(pad_end - TM, pad_counts, n_used, dest, u2.reshape(n, d), p_rows)
        y = _experts(block_exp, n_used, xs, w_gate[l].astype(BF16), w_up[l].astype(BF16), w_down[l].astype(BF16))
        x = _combine(alpha, dest, gate_w, x1.reshape(n, d), mod, ln2_g[l].reshape(1, d), ln2_b[l].reshape(1, d), y,
                     tiles_per_b).reshape(bsz, seq, d)
    return x
```

```python
import functools

import jax
import jax.numpy as jnp
from jax import lax
from jax.experimental import pallas as pl
from jax.experimental.pallas import tpu as pltpu

F32 = jnp.float32
BF16 = jnp.bfloat16

N_HEADS = 8
HEAD_DIM = 64
ATTN_W = N_HEADS * HEAD_DIM
N_EXPERTS = 16
N_GROUPS = 4
EXPERTS_PER_GROUP = N_EXPERTS // N_GROUPS
TOP_K = 2
CONV_K = 3
LN_EPS = 1e-5

T_INPROJ = 512
T_POST = 512
TK = 256
ATTN_HEADS_PER_STEP = 4
N_NEAR = 3
TM = 512
SUBLANES = 8
ADA_TN = 1536

R_STOP = 110.0

VMEM_LIMIT = 56 * 1024 * 1024
ATTN_FLAGS = None


def _cparams(sem, flags=None):
    return pltpu.CompilerParams(dimension_semantics=sem, vmem_limit_bytes=VMEM_LIMIT, flags=flags)


def _dot(a, b):
    return jnp.dot(a, b, preferred_element_type=F32)


def _dot_nt(a, b):
    return lax.dot_general(a, b, (((1,), (1,)), ((), ())), preferred_element_type=F32)


def _split_bf16(v):
    hi = v.astype(BF16)
    lo = (v - hi.astype(F32)).astype(BF16)
    return hi, lo


def _ada_kernel(c_ref, w_ref, b_ref, o_ref):
    c = c_ref[...]
    s = c * jax.nn.sigmoid(c)
    o_ref[0] = jnp.dot(s, w_ref[0], precision=lax.Precision.HIGHEST, preferred_element_type=F32) + b_ref[0]


def _ada(c_pad, w_ada, b_ada):
    depth, d, n6 = w_ada.shape
    rows = c_pad.shape[0]
    return pl.pallas_call(
        _ada_kernel,
        grid=(depth, n6 // ADA_TN),
        in_specs=[
            pl.BlockSpec((rows, d), lambda l, j: (0, 0)),
            pl.BlockSpec((1, d, ADA_TN), lambda l, j: (l, 0, j)),
            pl.BlockSpec((1, 1, ADA_TN), lambda l, j: (l, 0, j)),
        ],
        out_specs=pl.BlockSpec((1, rows, ADA_TN), lambda l, j: (l, 0, j)),
        out_shape=jax.ShapeDtypeStruct((depth, rows, n6), F32),
        compiler_params=_cparams(("arbitrary", "arbitrary")),
        name="adaln",
    )(c_pad, w_ada, b_ada.reshape(depth, 1, n6))


def _inproj_kernel(x_ref, mod_ref, wc_ref, wqk_ref, wvt_ref, wconv_ref, g_ref, q_ref, k_ref, vt_ref, carry_ref):
    j = pl.program_id(1)
    t = x_ref.shape[1]
    cch = wconv_ref.shape[1]
    tk = vt_ref.shape[4]

    @pl.when(j == 0)
    def _():
        carry_ref[...] = jnp.zeros_like(carry_ref)

    sh1 = mod_ref[0, 0:1, :]
    sc1 = mod_ref[0, 1:2, :]
    u = (x_ref[0] * (1.0 + sc1) + sh1).astype(BF16)

    pc = _dot(u, wc_ref[...])
    cb = pc[:, 0:cch]
    h = pc[:, cch:2 * cch] * pc[:, 2 * cch:3 * cch]
    prev = carry_ref[...]
    p1 = prev[7:8, :]
    p2 = prev[6:7, :]
    row = lax.broadcasted_iota(jnp.int32, h.shape, 0)
    hm1 = jnp.where(row == 0, p1, pltpu.roll(h, 1, 0))
    hm2 = jnp.where(row == 0, p2, jnp.where(row == 1, p1, pltpu.roll(h, 2, 0)))
    wcv = wconv_ref[...]
    conv = wcv[2:3, :] * h + wcv[1:2, :] * hm1 + wcv[0:1, :] * hm2
    g_ref[0] = (cb * conv).astype(BF16)
    carry_ref[...] = h[t - 8:t, :]

    qk = _dot(u, wqk_ref[...])
    scale = HEAD_DIM ** -0.5
    for hd in range(N_HEADS):
        lo = hd * HEAD_DIM
        q_ref[0, hd] = (qk[:, lo:lo + HEAD_DIM] * scale).astype(BF16)
        k_ref[0, hd] = qk[:, ATTN_W + lo:ATTN_W + lo + HEAD_DIM].astype(BF16)
    vt = _dot_nt(wvt_ref[...], u)
    for hd in range(N_HEADS):
        for cj in range(t // tk):
            vt_ref[0, hd, cj] = vt[hd * HEAD_DIM:(hd + 1) * HEAD_DIM, cj * tk:(cj + 1) * tk].astype(BF16)


def _inproj(x, mod, w_c, w_qk, w_vt, w_conv, tk):
    b, s, d = x.shape
    cch = w_conv.shape[1]
    t = min(T_INPROJ, s)
    hshape = jax.ShapeDtypeStruct((b, N_HEADS, s, HEAD_DIM), BF16)
    hspec = pl.BlockSpec((1, N_HEADS, t, HEAD_DIM), lambda i, j: (i, 0, j, 0))
    return pl.pallas_call(
        _inproj_kernel,
        grid=(b, s // t),
        in_specs=[
            pl.BlockSpec((1, t, d), lambda i, j: (i, j, 0)),
            pl.BlockSpec((1, 6, d), lambda i, j: (i, 0, 0)),
            pl.BlockSpec(w_c.shape, lambda i, j: (0, 0)),
            pl.BlockSpec(w_qk.shape, lambda i, j: (0, 0)),
            pl.BlockSpec(w_vt.shape, lambda i, j: (0, 0)),
            pl.BlockSpec(w_conv.shape, lambda i, j: (0, 0)),
        ],
        out_specs=[
            pl.BlockSpec((1, t, cch), lambda i, j: (i, j, 0)), hspec, hspec,
            pl.BlockSpec((1, N_HEADS, t // tk, HEAD_DIM, tk), lambda i, j: (i, 0, j, 0, 0)),
        ],
        out_shape=[
            jax.ShapeDtypeStruct((b, s, cch), BF16), hshape, hshape,
            jax.ShapeDtypeStruct((b, N_HEADS, s // tk, HEAD_DIM, tk), BF16),
        ],
        scratch_shapes=[pltpu.VMEM((8, cch), F32)],
        compiler_params=_cparams(("arbitrary", "arbitrary")),
        name="inproj",
    )(x, mod, w_c, w_qk, w_vt, w_conv)


def _neg_abs(z):
    bits = lax.bitcast_convert_type(z, jnp.uint32) | jnp.uint32(0x80000000)
    return lax.bitcast_convert_type(bits, F32)


def _attn_kernel(q_ref, k_ref, vt_ref, trit_ref, ot_ref, za_buf, zb_buf, h_buf, r_buf, acc_buf, sat_ref):
    qi = pl.program_id(2)
    tk = za_buf.shape[2]
    nhead = za_buf.shape[1]
    rows = lax.broadcasted_iota(jnp.int32, (tk, tk), 0)
    cols = lax.broadcasted_iota(jnp.int32, (tk, tk), 1)
    causal = rows < cols

    chunk = lambda dist: jnp.maximum(qi - dist, 0)
    all_heads = tuple(range(nhead))

    def step(sc, sp, sw):
        cs = {}
        if sw is not None:
            dw, heads_w, slotw = sw
            for s in heads_w:
                cs[s] = _dot(trit_ref[...], h_buf[slotw, s])
        if sc is not None:
            dc, heads_c, slotc = sc
            k0 = pl.multiple_of(chunk(dc) * tk, tk)
            for s in heads_c:
                za_buf[slotc, s] = _dot_nt(k_ref[0, s, pl.ds(k0, tk), :], q_ref[0, s])
        if sp is not None:
            dp, heads_p, slotp = sp
            for s in heads_p:
                zt = za_buf[slotp, s]
                v = jnp.maximum(zt, 0.0) + jnp.log(1.0 + jnp.exp(_neg_abs(zt)))
                zb_buf[slotp, s] = zt - v
                if isinstance(dp, int) and dp == 0:
                    v = jnp.where(causal, v, 0.0)
                h_buf[slotp, s] = v.astype(BF16)
        if sw is not None:
            cw = chunk(dw)
            on = None if isinstance(dw, int) and dw == 0 else (dw <= qi).astype(F32)
            for s in heads_w:
                r = r_buf[s]
                p = jnp.exp(zb_buf[slotw, s] - cs[s])
                if on is None:
                    p = jnp.where(causal, p, 0.0)
                scale = jnp.exp(-r)
                add = cs[s][0:1, :] + h_buf[slotw, s, 0:1, :].astype(F32)
                if on is not None:
                    scale = scale * on
                    add = add * on
                acc_buf[s] += _dot(vt_ref[0, s, cw], p.astype(BF16)) * scale
                r_buf[s] = r + add

    def saturated():
        return (jnp.min(r_buf[...]) >= R_STOP).astype(jnp.int32)

    r_buf[...] = jnp.zeros_like(r_buf)
    acc_buf[...] = jnp.zeros_like(acc_buf)

    items = [(d, (s,), d) for d in range(N_NEAR) for s in range(nhead)]
    at = lambda m: items[m] if 0 <= m < len(items) else None
    for m in range(len(items) + 2):
        step(at(m), at(m - 1), at(m - 2))

    sat_ref[0] = saturated()

    @pl.when(jnp.logical_and(qi >= N_NEAR, sat_ref[0] == 0))
    def _():
        p0, p1 = N_NEAR % 2, (N_NEAR + 1) % 2
        step((N_NEAR, all_heads, p0), None, None)
        step((N_NEAR + 1, all_heads, p1), (N_NEAR, all_heads, p0), None)

        def not_done(carry):
            j, done = carry
            return jnp.logical_and(N_NEAR + 2 * j <= qi, done == 0)

        def pair(carry):
            j, _ = carry
            d = N_NEAR + 2 * j
            step((d + 2, all_heads, p0), (d + 1, all_heads, p1), (d, all_heads, p0))
            sat_ref[0] = saturated()

            @pl.when(jnp.logical_and(d + 1 <= qi, sat_ref[0] == 0))
            def _():
                step((d + 3, all_heads, p1), (d + 2, all_heads, p0), (d + 1, all_heads, p1))
                sat_ref[0] = saturated()

            return j + 1, sat_ref[0]

        lax.while_loop(not_done, pair, (jnp.int32(0), jnp.int32(0)))

    for s in range(nhead):
        ot_ref[0, s] = acc_buf[s].astype(BF16)


def _attention(q, k, vt, trit):
    b, nh, s, hd = q.shape
    tk = trit.shape[0]
    g = ATTN_HEADS_PER_STEP
    return pl.pallas_call(
        _attn_kernel,
        grid=(b, nh // g, s // tk),
        in_specs=[
            pl.BlockSpec((1, g, tk, hd), lambda i, h, j: (i, h, j, 0)),
            pl.BlockSpec((1, g, s, hd), lambda i, h, j: (i, h, 0, 0)),
            pl.BlockSpec((1, g, s // tk, hd, tk), lambda i, h, j: (i, h, 0, 0, 0)),
            pl.BlockSpec(trit.shape, lambda i, h, j: (0, 0)),
        ],
        out_specs=pl.BlockSpec((1, g, hd, tk), lambda i, h, j: (i, h, 0, j)),
        out_shape=jax.ShapeDtypeStruct((b, nh, hd, s), BF16),
        scratch_shapes=[
            pltpu.VMEM((N_NEAR, g, tk, tk), F32),
            pltpu.VMEM((N_NEAR, g, tk, tk), F32),
            pltpu.VMEM((N_NEAR, g, tk, tk), BF16),
            pltpu.VMEM((g, 1, tk), F32),
            pltpu.VMEM((g, hd, tk), F32),
            pltpu.SMEM((1,), jnp.int32),
        ],
        compiler_params=_cparams(("arbitrary", "arbitrary", "arbitrary"), ATTN_FLAGS),
        name="attn",
    )(q, k, vt, trit)


def _layer_norm(h, g, b):
    mu = jnp.mean(h, axis=-1, keepdims=True)
    hc = h - mu
    var = jnp.mean(hc * hc, axis=-1, keepdims=True)
    return hc * lax.rsqrt(var + LN_EPS) * g + b


def _first_argmax(vals):
    best = vals[0]
    idx = jnp.zeros(best.shape, jnp.int32)
    for j in range(1, len(vals)):
        upd = vals[j] > best
        best = jnp.where(upd, vals[j], best)
        idx = jnp.where(upd, j, idx)
    return best, idx


def _pick(idx, vals):
    out = vals[0]
    for j in range(1, len(vals)):
        out = jnp.where(idx == j, vals[j], out)
    return out


def _post_kernel(alpha, x_ref, g_ref, o_ref, mod_ref, wg_ref, wa_ref, wb_ref, wo_ref, lng_ref, lnb_ref,
                 wrt_ref, br_ref, su_ref,
                 x1_ref, u2_ref, ri_ref, rw_ref, cnt_ref, carry_ref):
    first = jnp.logical_and(pl.program_id(0) == 0, pl.program_id(1) == 0)

    @pl.when(first)
    def _():
        carry_ref[...] = jnp.zeros_like(carry_ref)

    d = x_ref.shape[2]
    x = x_ref[0]
    sh1 = mod_ref[0, 0:1, :]
    sc1 = mod_ref[0, 1:2, :]
    g1 = mod_ref[0, 2:3, :]
    sh2 = mod_ref[0, 3:4, :]
    sc2 = mod_ref[0, 4:5, :]
    u = (x * (1.0 + sc1) + sh1).astype(BF16)
    gates = _dot(u, wg_ref[...])
    y_a = _dot(g_ref[0], wa_ref[...])
    o_t = o_ref[0].reshape(ATTN_W, x.shape[0])
    y_b = lax.dot_general(o_t, wb_ref[...], (((0,), (0,)), ((), ())), preferred_element_type=F32)
    merged = jax.nn.sigmoid(gates[:, 0:d]) * y_a + jax.nn.sigmoid(gates[:, d:2 * d]) * y_b
    mix = _dot(merged.astype(BF16), wo_ref[...])
    x1 = _layer_norm(alpha * x + g1 * mix, lng_ref[...], lnb_ref[...])
    x1_ref[0] = x1
    u2 = x1 * (1.0 + sc2) + sh2
    u2_ref[0] = u2

    uh, ul = _split_bf16(u2)
    wh, wl = _split_bf16(wrt_ref[...])
    logits = _dot_nt(wh, uh) + _dot_nt(wh, ul) + _dot_nt(wl, uh)
    aff = jax.nn.sigmoid(logits)
    sel = aff + br_ref[...]
    selr = [sel[e:e + 1, :] for e in range(N_EXPERTS)]
    affr = [aff[e:e + 1, :] for e in range(N_EXPERTS)]
    gscore = []
    for gi in range(N_GROUPS):
        m = selr[gi * EXPERTS_PER_GROUP:(gi + 1) * EXPERTS_PER_GROUP]
        best = None
        for a in range(EXPERTS_PER_GROUP):
            for bb in range(a + 1, EXPERTS_PER_GROUP):
                pair = m[a] + m[bb]
                best = pair if best is None else jnp.maximum(best, pair)
        gscore.append(best)
    _, gidx = _first_argmax(gscore)
    sel_in = [_pick(gidx, [selr[gi * EXPERTS_PER_GROUP + j] for gi in range(N_GROUPS)]) for j in range(EXPERTS_PER_GROUP)]
    aff_in = [_pick(gidx, [affr[gi * EXPERTS_PER_GROUP + j] for gi in range(N_GROUPS)]) for j in range(EXPERTS_PER_GROUP)]
    _, i1 = _first_argmax(sel_in)
    neg = jnp.full_like(sel_in[0], -jnp.inf)
    _, i2 = _first_argmax([jnp.where(i1 == j, neg, sel_in[j]) for j in range(EXPERTS_PER_GROUP)])
    a1 = _pick(i1, aff_in)
    a2 = _pick(i2, aff_in)
    den = a1 + a2
    e1 = gidx * EXPERTS_PER_GROUP + i1
    e2 = gidx * EXPERTS_PER_GROUP + i2

    eio = lax.broadcasted_iota(jnp.int32, logits.shape, 0)
    hit1 = eio == e1
    hit2 = eio == e2
    member = jnp.where(jnp.logical_or(hit1, hit2), 1.0, 0.0)
    before = _dot(member.astype(BF16), su_ref[...]) + carry_ref[:, 0:1]
    rank1 = jnp.sum(jnp.where(hit1, before, 0.0), axis=0, keepdims=True)
    rank2 = jnp.sum(jnp.where(hit2, before, 0.0), axis=0, keepdims=True)
    t = member.shape[1]
    total = before[:, t - 1:t] + member[:, t - 1:t]
    carry_ref[...] = jnp.broadcast_to(total, carry_ref.shape)
    cnt_ref[...] = jnp.broadcast_to(total, cnt_ref.shape)

    zi = jnp.zeros_like(e1)
    ri_ref[0] = jnp.concatenate([e1, e2, rank1.astype(jnp.int32), rank2.astype(jnp.int32), zi, zi, zi, zi], axis=0)
    zf = jnp.zeros_like(a1)
    rw_ref[0] = jnp.concatenate([a1 / den, a2 / den, zf, zf, zf, zf, zf, zf], axis=0)


def _post(alpha, x, g, o, mod, w_g, w_a, w_b, w_o, ln_g, ln_b, w_rt, b_r, su):
    b, s, d = x.shape
    t = su.shape[0]
    nt = (b * s) // t
    tiles_per_b = s // t
    full2 = lambda a: pl.BlockSpec(a.shape, lambda i, j: (0, 0))
    tile_idx = lambda i, j: (i * tiles_per_b + j, 0, 0)
    return pl.pallas_call(
        functools.partial(_post_kernel, alpha),
        grid=(b, tiles_per_b),
        in_specs=[
            pl.BlockSpec((1, t, d), lambda i, j: (i, j, 0)),
            pl.BlockSpec((1, t, g.shape[2]), lambda i, j: (i, j, 0)),
            pl.BlockSpec((1, N_HEADS, HEAD_DIM, t), lambda i, j: (i, 0, 0, j)),
            pl.BlockSpec((1, 6, d), lambda i, j: (i, 0, 0)),
            full2(w_g), full2(w_a), full2(w_b), full2(w_o), full2(ln_g), full2(ln_b), full2(w_rt), full2(b_r), full2(su),
        ],
        out_specs=[
            pl.BlockSpec((1, t, d), lambda i, j: (i, j, 0)),
            pl.BlockSpec((1, t, d), lambda i, j: (i, j, 0)),
            pl.BlockSpec((1, 8, t), tile_idx),
            pl.BlockSpec((1, 8, t), tile_idx),
            pl.BlockSpec((N_EXPERTS, 128), lambda i, j: (0, 0)),
        ],
        out_shape=[
            jax.ShapeDtypeStruct((b, s, d), F32),
            jax.ShapeDtypeStruct((b, s, d), F32),
            jax.ShapeDtypeStruct((nt, 8, t), jnp.int32),
            jax.ShapeDtypeStruct((nt, 8, t), F32),
            jax.ShapeDtypeStruct((N_EXPERTS, 128), F32),
        ],
        scratch_shapes=[pltpu.VMEM((N_EXPERTS, 128), F32)],
        compiler_params=_cparams(("arbitrary", "arbitrary")),
        name="post",
    )(x, g, o, mod, w_g, w_a, w_b, w_o, ln_g, ln_b, w_rt, b_r, su)


def _tile_row(ref, row):
    return ref.at[lax.shift_right_logical(row, 3), pl.ds(row & (SUBLANES - 1), 1)]


def _dispatch_kernel(last_ref, used_ref, nu_ref, dest_ref, u2_ref, xs_ref, zero_buf, sem, zsem):
    t = u2_ref.shape[0] * SUBLANES
    tm8 = zero_buf.shape[0]

    @pl.when(pl.program_id(0) == 0)
    def _():
        zero_buf[...] = jnp.zeros_like(zero_buf)

        def zero_block(block):
            cp = pltpu.make_async_copy(zero_buf, xs_ref.at[pl.ds(block * tm8, tm8)], zsem)
            cp.start()
            cp.wait()

        for e in range(N_EXPERTS):
            @pl.when(used_ref[e] > 0)
            def _():
                zero_block(last_ref[e])

        def tail(b, c):
            zero_block(b)
            return c

        lax.fori_loop(nu_ref[0], xs_ref.shape[0] // tm8, tail, 0)

    def issue(g, c):
        for u in range(SUBLANES):
            src = u2_ref.at[g, pl.ds(u, 1)]
            for k in range(TOP_K):
                dst = _tile_row(xs_ref, dest_ref[0, 0, k * t + g * SUBLANES + u])
                pltpu.make_async_copy(src, dst, sem).start(priority=(u + k) % 2)
        return c

    lax.fori_loop(0, t // SUBLANES, issue, 0)
    for _ in range(TOP_K):
        pltpu.make_async_copy(u2_ref, xs_ref.at[pl.ds(0, t // SUBLANES)], sem).wait()


def _dispatch(last_block, seg_used, n_used, dest, u2, p_rows):
    n8, _, d = u2.shape
    nt = dest.shape[0]
    t = n8 * SUBLANES // nt
    grid_spec = pltpu.PrefetchScalarGridSpec(
        num_scalar_prefetch=3,
        grid=(nt,),
        in_specs=[
            pl.BlockSpec((1, 1, 2 * t), lambda i, lb, us, nu: (i, 0, 0), memory_space=pltpu.SMEM),
            pl.BlockSpec((t // SUBLANES, SUBLANES, d), lambda i, lb, us, nu: (i, 0, 0)),
        ],
        out_specs=pl.BlockSpec(memory_space=pl.ANY),
        scratch_shapes=[pltpu.VMEM((TM // SUBLANES, SUBLANES, d), F32), pltpu.SemaphoreType.DMA(()),
                        pltpu.SemaphoreType.DMA(())],
    )
    return pl.pallas_call(
        _dispatch_kernel,
        grid_spec=grid_spec,
        out_shape=jax.ShapeDtypeStruct((p_rows // SUBLANES, SUBLANES, d), F32),
        compiler_params=_cparams(("arbitrary",)),
        name="dispatch",
    )(last_block, seg_used, n_used, dest, u2)


def _expert_kernel(be_ref, nu_ref, xs_ref, wg_ref, wu_ref, wd_ref, y_ref):
    i = pl.program_id(0)

    @pl.when(i < nu_ref[0])
    def _():
        xb = xs_ref[...].astype(BF16)
        hg = _dot(xb, wg_ref[0])
        hu = _dot(xb, wu_ref[0])
        h = (hg * jax.nn.sigmoid(hg) * hu).astype(BF16)
        y_ref[...] = _dot(h, wd_ref[0])

    @pl.when(i >= nu_ref[0])
    def _():
        y_ref[...] = jnp.zeros_like(y_ref)


def _experts(block_exp, n_used, xs, w_gate, w_up, w_down):
    p, d = xs.shape
    ff = w_gate.shape[2]
    nb = p // TM
    grid_spec = pltpu.PrefetchScalarGridSpec(
        num_scalar_prefetch=2,
        grid=(nb,),
        in_specs=[
            pl.BlockSpec((TM, d), lambda i, be, nu: (jnp.minimum(i, nu[0] - 1), 0)),
            pl.BlockSpec((1, d, ff), lambda i, be, nu: (be[i], 0, 0)),
            pl.BlockSpec((1, d, ff), lambda i, be, nu: (be[i], 0, 0)),
            pl.BlockSpec((1, ff, d), lambda i, be, nu: (be[i], 0, 0)),
        ],
        out_specs=pl.BlockSpec((TM, d), lambda i, be, nu: (i, 0)),
    )
    return pl.pallas_call(
        _expert_kernel,
        grid_spec=grid_spec,
        out_shape=jax.ShapeDtypeStruct((p, d), F32),
        compiler_params=_cparams(("arbitrary",)),
        name="experts",
    )(block_exp, n_used, xs, w_gate, w_up, w_down)


def _combine_kernel(alpha, dest_ref, dnext_ref, w_ref, x1_ref, mod_ref, lng_ref, lnb_ref, y_ref, o_ref, ybuf, sems):
    i = pl.program_id(0)
    t = x1_ref.shape[0]
    slot = lax.rem(i, 2)

    def gather(dref, sl):
        def issue(g, c):
            for u in range(SUBLANES):
                for k in range(TOP_K):
                    src = _tile_row(y_ref, dref[0, 0, k * t + g * SUBLANES + u])
                    pltpu.make_async_copy(src, ybuf.at[sl, k, g, pl.ds(u, 1)], sems.at[sl]).start(priority=(u + k) % 2)
            return c

        lax.fori_loop(0, t // SUBLANES, issue, 0)

    @pl.when(i == 0)
    def _():
        gather(dest_ref, 0)

    @pl.when(i + 1 < pl.num_programs(0))
    def _():
        gather(dnext_ref, 1 - slot)

    for k in range(TOP_K):
        pltpu.make_async_copy(y_ref.at[pl.ds(0, t // SUBLANES)], ybuf.at[slot, k], sems.at[slot]).wait()

    g2 = mod_ref[0, 5:6, :]
    d = x1_ref.shape[1]
    ffn = w_ref[:, 0:1] * ybuf[slot, 0].reshape(t, d) + w_ref[:, 1:2] * ybuf[slot, 1].reshape(t, d)
    o_ref[...] = _layer_norm(alpha * x1_ref[...] + g2 * ffn, lng_ref[...], lnb_ref[...])


def _combine(alpha, dest, w, x1, mod, ln_g, ln_b, y, tiles_per_b):
    n, d = x1.shape
    nt = dest.shape[0]
    t = n // nt
    return pl.pallas_call(
        functools.partial(_combine_kernel, alpha),
        grid=(nt,),
        in_specs=[
            pl.BlockSpec((1, 1, 2 * t), lambda i: (i, 0, 0), memory_space=pltpu.SMEM),
            pl.BlockSpec((1, 1, 2 * t), lambda i: (jnp.minimum(i + 1, nt - 1), 0, 0), memory_space=pltpu.SMEM),
            pl.BlockSpec((t, 2), lambda i: (i, 0)),
            pl.BlockSpec((t, d), lambda i: (i, 0)),
            pl.BlockSpec((1, 6, d), lambda i: (i // tiles_per_b, 0, 0)),
            pl.BlockSpec(ln_g.shape, lambda i: (0, 0)),
            pl.BlockSpec(ln_b.shape, lambda i: (0, 0)),
            pl.BlockSpec(memory_space=pl.ANY),
        ],
        out_specs=pl.BlockSpec((t, d), lambda i: (i, 0)),
        out_shape=jax.ShapeDtypeStruct((n, d), F32),
        scratch_shapes=[pltpu.VMEM((2, TOP_K, t // SUBLANES, SUBLANES, d), F32), pltpu.SemaphoreType.DMA((2,))],
        compiler_params=_cparams(("arbitrary",)),
        name="combine",
    )(dest, dest, w, x1, mod, ln_g, ln_b, y)


def kernel(x, c, w_ada, b_ada, w_in, w_conv, w_a, w_b, w_o, ln1_g, ln1_b, w_router, b_router, w_gate, w_up, w_down,
           ln2_g, ln2_b):
    bsz, seq, d = x.shape
    depth = w_ada.shape[0]
    n = bsz * seq
    cch = w_conv.shape[2]
    alpha = float((2 * depth) ** 0.25)
    n_conv = 3 * cch
    n_qkv = 3 * ATTN_W

    c_pad = jnp.zeros((8, d), F32).at[:bsz].set(c)
    mod_all = _ada(c_pad, w_ada, b_ada)

    tk = min(TK, seq)
    ii = jnp.arange(tk, dtype=jnp.int32)
    trit = (ii[None, :] > ii[:, None]).astype(BF16)
    t_post = min(T_POST, seq)
    jj = jnp.arange(t_post, dtype=jnp.int32)
    su = (jj[:, None] < jj[None, :]).astype(BF16)
    w_rt = w_router.T
    b_r = b_router.reshape(N_EXPERTS, 1)

    a_rows = n * TOP_K
    n_blocks = -(-a_rows // TM) + N_EXPERTS
    p_rows = n_blocks * TM
    tiles_per_b = seq // t_post

    for l in range(depth):
        mod = mod_all[l, :bsz].reshape(bsz, 6, d)
        w_in_l = w_in[l].astype(BF16)
        n_qk = 2 * ATTN_W
        g, q, k, vt = _inproj(x, mod, w_in_l[:, :n_conv], w_in_l[:, n_conv:n_conv + n_qk],
                              w_in_l[:, n_conv + n_qk:n_conv + n_qkv].T, w_conv[l], tk)
        o = _attention(q, k, vt, trit)
        x1, u2, ri, rw, cnt = _post(
            alpha, x, g, o, mod, w_in_l[:, n_conv + n_qkv:], w_a[l].astype(BF16), w_b[l].astype(BF16),
            w_o[l].astype(BF16), ln1_g[l].reshape(1, d), ln1_b[l].reshape(1, d), w_rt, b_r, su)

        counts = cnt[:, 0].astype(jnp.int32)
        pad_counts = (counts + TM - 1) // TM * TM
        pad_end = jnp.cumsum(pad_counts)
        pad_start = pad_end - pad_counts
        e12 = ri[:, 0:2, :]
        e_ids = jnp.arange(N_EXPERTS, dtype=jnp.int32)
        seg_start = jnp.sum(jnp.where(e12[..., None] == e_ids, pad_start, 0), axis=-1)
        dest = (seg_start + ri[:, 2:4, :]).reshape(-1, 1, 2 * t_post)
        block_start = jnp.arange(n_blocks, dtype=jnp.int32) * TM
        block_exp = jnp.minimum(jnp.sum((block_start[:, None] >= pad_end[None, :]).astype(jnp.int32), axis=1),
                                N_EXPERTS - 1)
        n_used = (pad_end[N_EXPERTS - 1] // TM).astype(jnp.int32).reshape(1)
        gate_w = jnp.transpose(rw[:, 0:2, :], (0, 2, 1)).reshape(n, 2)

        xs = _dispatch(pad_end // TM - 1, pad_counts, n_used, dest, u2.reshape(n // SUBLANES, SUBLANES, d), p_rows)
        y = _experts(block_exp, n_used, xs.reshape(p_rows, d), w_gate[l].astype(BF16), w_up[l].astype(BF16),
                     w_down[l].astype(BF16))
        x = _combine(alpha, dest, gate_w, x1.reshape(n, d), mod, ln2_g[l].reshape(1, d), ln2_b[l].reshape(1, d),
                     y.reshape(p_rows // SUBLANES, SUBLANES, d), tiles_per_b).reshape(bsz, seq, d)
    return x
```

```python
import functools

import jax
import jax.numpy as jnp
from jax import lax
from jax.experimental import pallas as pl
from jax.experimental.pallas import tpu as pltpu

F32 = jnp.float32
BF16 = jnp.bfloat16

N_HEADS = 8
HEAD_DIM = 64
ATTN_W = N_HEADS * HEAD_DIM
N_EXPERTS = 16
N_GROUPS = 4
EXPERTS_PER_GROUP = N_EXPERTS // N_GROUPS
TOP_K = 2
PAIR_LO = (0, 0, 0, 1, 1, 2)
PAIR_HI = (1, 2, 3, 2, 3, 3)
N_CLASSES = N_GROUPS * len(PAIR_LO)
CLASS_ROWS = 32
CONV_K = 3
LN_EPS = 1e-5

T_INPROJ = 512
T_POST = 512
TK = 256
ATTN_HEADS_PER_STEP = 8
N_NEAR = 3
TM = 512
SUBLANES = 8
ADA_TN = 1536

R_STOP = 110.0

VMEM_LIMIT = 56 * 1024 * 1024
ATTN_FLAGS = None


def _cparams(sem, flags=None):
    return pltpu.CompilerParams(dimension_semantics=sem, vmem_limit_bytes=VMEM_LIMIT, flags=flags)


def _dot(a, b):
    return jnp.dot(a, b, preferred_element_type=F32)


def _dot_nt(a, b):
    return lax.dot_general(a, b, (((1,), (1,)), ((), ())), preferred_element_type=F32)


def _split_bf16(v):
    hi = v.astype(BF16)
    lo = (v - hi.astype(F32)).astype(BF16)
    return hi, lo


def _ada_kernel(c_ref, w_ref, b_ref, o_ref):
    c = c_ref[...]
    s = c * jax.nn.sigmoid(c)
    o_ref[0] = jnp.dot(s, w_ref[0], precision=lax.Precision.HIGHEST, preferred_element_type=F32) + b_ref[0]


def _ada(c_pad, w_ada, b_ada):
    depth, d, n6 = w_ada.shape
    rows = c_pad.shape[0]
    return pl.pallas_call(
        _ada_kernel,
        grid=(depth, n6 // ADA_TN),
        in_specs=[
            pl.BlockSpec((rows, d), lambda l, j: (0, 0)),
            pl.BlockSpec((1, d, ADA_TN), lambda l, j: (l, 0, j)),
            pl.BlockSpec((1, 1, ADA_TN), lambda l, j: (l, 0, j)),
        ],
        out_specs=pl.BlockSpec((1, rows, ADA_TN), lambda l, j: (l, 0, j)),
        out_shape=jax.ShapeDtypeStruct((depth, rows, n6), F32),
        compiler_params=_cparams(("arbitrary", "arbitrary")),
        name="adaln",
    )(c_pad, w_ada, b_ada.reshape(depth, 1, n6))


def _inproj_kernel(x_ref, mod_ref, wc_ref, wqk_ref, wvt_ref, wconv_ref, g_ref, q_ref, k_ref, vt_ref, carry_ref):
    j = pl.program_id(1)
    t = x_ref.shape[1]
    cch = wconv_ref.shape[1]
    tk = vt_ref.shape[4]

    @pl.when(j == 0)
    def _():
        carry_ref[...] = jnp.zeros_like(carry_ref)

    sh1 = mod_ref[0, 0:1, :]
    sc1 = mod_ref[0, 1:2, :]
    u = (x_ref[0] * (1.0 + sc1) + sh1).astype(BF16)

    pc = _dot(u, wc_ref[...])
    cb = pc[:, 0:cch]
    h = pc[:, cch:2 * cch] * pc[:, 2 * cch:3 * cch]
    prev = carry_ref[...]
    p1 = prev[7:8, :]
    p2 = prev[6:7, :]
    row = lax.broadcasted_iota(jnp.int32, h.shape, 0)
    hm1 = jnp.where(row == 0, p1, pltpu.roll(h, 1, 0))
    hm2 = jnp.where(row == 0, p2, jnp.where(row == 1, p1, pltpu.roll(h, 2, 0)))
    wcv = wconv_ref[...]
    conv = wcv[2:3, :] * h + wcv[1:2, :] * hm1 + wcv[0:1, :] * hm2
    g_ref[0] = (cb * conv).astype(BF16)
    carry_ref[...] = h[t - 8:t, :]

    qk = _dot(u, wqk_ref[...])
    scale = HEAD_DIM ** -0.5
    for hd in range(N_HEADS):
        lo = hd * HEAD_DIM
        q_ref[0, hd] = (qk[:, lo:lo + HEAD_DIM] * scale).astype(BF16)
        k_ref[0, hd] = qk[:, ATTN_W + lo:ATTN_W + lo + HEAD_DIM].astype(BF16)
    vt = _dot_nt(wvt_ref[...], u)
    for hd in range(N_HEADS):
        for cj in range(t // tk):
            vt_ref[0, hd, cj] = vt[hd * HEAD_DIM:(hd + 1) * HEAD_DIM, cj * tk:(cj + 1) * tk].astype(BF16)


def _inproj(x, mod, w_c, w_qk, w_vt, w_conv, tk):
    b, s, d = x.shape
    cch = w_conv.shape[1]
    t = min(T_INPROJ, s)
    hshape = jax.ShapeDtypeStruct((b, N_HEADS, s, HEAD_DIM), BF16)
    hspec = pl.BlockSpec((1, N_HEADS, t, HEAD_DIM), lambda i, j: (i, 0, j, 0))
    return pl.pallas_call(
        _inproj_kernel,
        grid=(b, s // t),
        in_specs=[
            pl.BlockSpec((1, t, d), lambda i, j: (i, j, 0)),
            pl.BlockSpec((1, 6, d), lambda i, j: (i, 0, 0)),
            pl.BlockSpec(w_c.shape, lambda i, j: (0, 0)),
            pl.BlockSpec(w_qk.shape, lambda i, j: (0, 0)),
            pl.BlockSpec(w_vt.shape, lambda i, j: (0, 0)),
            pl.BlockSpec(w_conv.shape, lambda i, j: (0, 0)),
        ],
        out_specs=[
            pl.BlockSpec((1, t, cch), lambda i, j: (i, j, 0)), hspec, hspec,
            pl.BlockSpec((1, N_HEADS, t // tk, HEAD_DIM, tk), lambda i, j: (i, 0, j, 0, 0)),
        ],
        out_shape=[
            jax.ShapeDtypeStruct((b, s, cch), BF16), hshape, hshape,
            jax.ShapeDtypeStruct((b, N_HEADS, s // tk, HEAD_DIM, tk), BF16),
        ],
        scratch_shapes=[pltpu.VMEM((8, cch), F32)],
        compiler_params=_cparams(("arbitrary", "arbitrary")),
        name="inproj",
    )(x, mod, w_c, w_qk, w_vt, w_conv)


def _neg_abs(z):
    bits = lax.bitcast_convert_type(z, jnp.uint32) | jnp.uint32(0x80000000)
    return lax.bitcast_convert_type(bits, F32)


def _attn_kernel(q_ref, k_ref, vt_ref, trit_ref, ot_ref, za_buf, zb_buf, h_buf, r_buf, acc_buf, sat_ref):
    qi = pl.program_id(2)
    tk = za_buf.shape[2]
    nhead = za_buf.shape[1]
    rows = lax.broadcasted_iota(jnp.int32, (tk, tk), 0)
    cols = lax.broadcasted_iota(jnp.int32, (tk, tk), 1)
    causal = rows < cols

    chunk = lambda dist: jnp.maximum(qi - dist, 0)
    all_heads = tuple(range(nhead))

    def step(sc, sp, sw):
        cs = {}
        if sw is not None:
            dw, heads_w, slotw = sw
            for s in heads_w:
                cs[s] = _dot(trit_ref[...], h_buf[slotw, s])
        if sc is not None:
            dc, heads_c, slotc = sc
            k0 = pl.multiple_of(chunk(dc) * tk, tk)
            for s in heads_c:
                za_buf[slotc, s] = _dot_nt(k_ref[0, s, pl.ds(k0, tk), :], q_ref[0, s])
        if sp is not None:
            dp, heads_p, slotp = sp
            for s in heads_p:
                zt = za_buf[slotp, s]
                v = jnp.maximum(zt, 0.0) + jnp.log(1.0 + jnp.exp(_neg_abs(zt)))
                zb_buf[slotp, s] = zt - v
                if isinstance(dp, int) and dp == 0:
                    v = jnp.where(causal, v, 0.0)
                h_buf[slotp, s] = v.astype(BF16)
        if sw is not None:
            cw = chunk(dw)
            on = None if isinstance(dw, int) and dw == 0 else (dw <= qi).astype(F32)
            for s in heads_w:
                r = r_buf[s]
                p = jnp.exp(zb_buf[slotw, s] - cs[s])
                if on is None:
                    p = jnp.where(causal, p, 0.0)
                scale = jnp.exp(-r)
                add = cs[s][0:1, :] + h_buf[slotw, s, 0:1, :].astype(F32)
                if on is not None:
                    scale = scale * on
                    add = add * on
                acc_buf[s] += _dot(vt_ref[0, s, cw], p.astype(BF16)) * scale
                r_buf[s] = r + add

    def saturated():
        return (jnp.min(r_buf[...]) >= R_STOP).astype(jnp.int32)

    r_buf[...] = jnp.zeros_like(r_buf)
    acc_buf[...] = jnp.zeros_like(acc_buf)

    items = [(d, (s,), d) for d in range(N_NEAR) for s in range(nhead)]
    at = lambda m: items[m] if 0 <= m < len(items) else None
    for m in range(len(items) + 2):
        step(at(m), at(m - 1), at(m - 2))

    sat_ref[0] = saturated()

    @pl.when(jnp.logical_and(qi >= N_NEAR, sat_ref[0] == 0))
    def _():
        p0, p1 = N_NEAR % 2, (N_NEAR + 1) % 2
        step((N_NEAR, all_heads, p0), None, None)
        step((N_NEAR + 1, all_heads, p1), (N_NEAR, all_heads, p0), None)

        def not_done(carry):
            j, done = carry
            return jnp.logical_and(N_NEAR + 2 * j <= qi, done == 0)

        def pair(carry):
            j, _ = carry
            d = N_NEAR + 2 * j
            step((d + 2, all_heads, p0), (d + 1, all_heads, p1), (d, all_heads, p0))
            sat_ref[0] = saturated()

            @pl.when(jnp.logical_and(d + 1 <= qi, sat_ref[0] == 0))
            def _():
                step((d + 3, all_heads, p1), (d + 2, all_heads, p0), (d + 1, all_heads, p1))
                sat_ref[0] = saturated()

            return j + 1, sat_ref[0]

        lax.while_loop(not_done, pair, (jnp.int32(0), jnp.int32(0)))

    for s in range(nhead):
        ot_ref[0, s] = acc_buf[s].astype(BF16)


def _attention(q, k, vt, trit):
    b, nh, s, hd = q.shape
    tk = trit.shape[0]
    g = ATTN_HEADS_PER_STEP
    return pl.pallas_call(
        _attn_kernel,
        grid=(b, nh // g, s // tk),
        in_specs=[
            pl.BlockSpec((1, g, tk, hd), lambda i, h, j: (i, h, j, 0)),
            pl.BlockSpec((1, g, s, hd), lambda i, h, j: (i, h, 0, 0), pipeline_mode=pl.Buffered(1)),
            pl.BlockSpec((1, g, s // tk, hd, tk), lambda i, h, j: (i, h, 0, 0, 0), pipeline_mode=pl.Buffered(1)),
            pl.BlockSpec(trit.shape, lambda i, h, j: (0, 0)),
        ],
        out_specs=pl.BlockSpec((1, g, hd, tk), lambda i, h, j: (i, h, 0, j)),
        out_shape=jax.ShapeDtypeStruct((b, nh, hd, s), BF16),
        scratch_shapes=[
            pltpu.VMEM((N_NEAR, g, tk, tk), F32),
            pltpu.VMEM((N_NEAR, g, tk, tk), F32),
            pltpu.VMEM((N_NEAR, g, tk, tk), BF16),
            pltpu.VMEM((g, 1, tk), F32),
            pltpu.VMEM((g, hd, tk), F32),
            pltpu.SMEM((1,), jnp.int32),
        ],
        compiler_params=_cparams(("arbitrary", "arbitrary", "arbitrary"), ATTN_FLAGS),
        name="attn",
    )(q, k, vt, trit)


def _layer_norm(h, g, b):
    mu = jnp.mean(h, axis=-1, keepdims=True)
    hc = h - mu
    var = jnp.mean(hc * hc, axis=-1, keepdims=True)
    return hc * lax.rsqrt(var + LN_EPS) * g + b


def _first_argmax(vals):
    best = vals[0]
    idx = jnp.zeros(best.shape, jnp.int32)
    for j in range(1, len(vals)):
        upd = vals[j] > best
        best = jnp.where(upd, vals[j], best)
        idx = jnp.where(upd, j, idx)
    return best, idx


def _pick(idx, vals):
    out = vals[0]
    for j in range(1, len(vals)):
        out = jnp.where(idx == j, vals[j], out)
    return out


def _post_kernel(alpha, x_ref, g_ref, o_ref, mod_ref, wg_ref, wa_ref, wb_ref, wo_ref, lng_ref, lnb_ref,
                 wrt_ref, br_ref, su_ref,
                 x1_ref, u2_ref, ri_ref, rw_ref, cnt_ref, carry_ref):
    first = jnp.logical_and(pl.program_id(0) == 0, pl.program_id(1) == 0)

    @pl.when(first)
    def _():
        carry_ref[...] = jnp.zeros_like(carry_ref)

    d = x_ref.shape[2]
    x = x_ref[0]
    sh1 = mod_ref[0, 0:1, :]
    sc1 = mod_ref[0, 1:2, :]
    g1 = mod_ref[0, 2:3, :]
    sh2 = mod_ref[0, 3:4, :]
    sc2 = mod_ref[0, 4:5, :]
    u = (x * (1.0 + sc1) + sh1).astype(BF16)
    gates = _dot(u, wg_ref[...])
    y_a = _dot(g_ref[0], wa_ref[...])
    o_t = o_ref[0].reshape(ATTN_W, x.shape[0])
    y_b = lax.dot_general(o_t, wb_ref[...], (((0,), (0,)), ((), ())), preferred_element_type=F32)
    merged = jax.nn.sigmoid(gates[:, 0:d]) * y_a + jax.nn.sigmoid(gates[:, d:2 * d]) * y_b
    mix = _dot(merged.astype(BF16), wo_ref[...])
    x1 = _layer_norm(alpha * x + g1 * mix, lng_ref[...], lnb_ref[...])
    x1_ref[0] = x1
    u2 = x1 * (1.0 + sc2) + sh2
    u2_ref[0] = u2

    uh, ul = _split_bf16(u2)
    wh, wl = _split_bf16(wrt_ref[...])
    logits = _dot_nt(wh, uh) + _dot_nt(wh, ul) + _dot_nt(wl, uh)
    aff = jax.nn.sigmoid(logits)
    sel = aff + br_ref[...]
    selr = [sel[e:e + 1, :] for e in range(N_EXPERTS)]
    affr = [aff[e:e + 1, :] for e in range(N_EXPERTS)]
    gscore = []
    for gi in range(N_GROUPS):
        m = selr[gi * EXPERTS_PER_GROUP:(gi + 1) * EXPERTS_PER_GROUP]
        best = None
        for a in range(EXPERTS_PER_GROUP):
            for bb in range(a + 1, EXPERTS_PER_GROUP):
                pair = m[a] + m[bb]
                best = pair if best is None else jnp.maximum(best, pair)
        gscore.append(best)
    _, gidx = _first_argmax(gscore)
    sel_in = [_pick(gidx, [selr[gi * EXPERTS_PER_GROUP + j] for gi in range(N_GROUPS)]) for j in range(EXPERTS_PER_GROUP)]
    aff_in = [_pick(gidx, [affr[gi * EXPERTS_PER_GROUP + j] for gi in range(N_GROUPS)]) for j in range(EXPERTS_PER_GROUP)]
    _, i1 = _first_argmax(sel_in)
    neg = jnp.full_like(sel_in[0], -jnp.inf)
    _, i2 = _first_argmax([jnp.where(i1 == j, neg, sel_in[j]) for j in range(EXPERTS_PER_GROUP)])
    a1 = _pick(i1, aff_in)
    a2 = _pick(i2, aff_in)
    den = a1 + a2

    lo = jnp.minimum(i1, i2)
    hi = jnp.maximum(i1, i2)
    pair = jnp.where(lo == 0, hi - 1, jnp.where(lo == 1, hi + 1, len(PAIR_LO) - 1))
    cls = gidx * len(PAIR_LO) + pair
    w1 = a1 / den
    w2 = a2 / den
    first_is_lo = i1 < i2
    w_lo = jnp.where(first_is_lo, w1, w2)
    w_hi = jnp.where(first_is_lo, w2, w1)

    cio = lax.broadcasted_iota(jnp.int32, (CLASS_ROWS, cls.shape[1]), 0)
    hit = cio == cls
    member = jnp.where(hit, 1.0, 0.0)
    before = _dot(member.astype(BF16), su_ref[...]) + carry_ref[:, 0:1]
    rank = jnp.sum(jnp.where(hit, before, 0.0), axis=0, keepdims=True)
    t = member.shape[1]
    total = before[:, t - 1:t] + member[:, t - 1:t]
    carry_ref[...] = jnp.broadcast_to(total, carry_ref.shape)
    cnt_ref[...] = jnp.broadcast_to(total, cnt_ref.shape)

    zi = jnp.zeros_like(cls)
    ri_ref[0] = jnp.concatenate([cls, rank.astype(jnp.int32), zi, zi, zi, zi, zi, zi], axis=0)
    zf = jnp.zeros_like(w_lo)
    rw_ref[0] = jnp.concatenate([w_lo, w_hi, zf, zf, zf, zf, zf, zf], axis=0)


def _post(alpha, x, g, o, mod, w_g, w_a, w_b, w_o, ln_g, ln_b, w_rt, b_r, su):
    b, s, d = x.shape
    t = su.shape[0]
    nt = (b * s) // t
    tiles_per_b = s // t
    full2 = lambda a: pl.BlockSpec(a.shape, lambda i, j: (0, 0))
    tile_idx = lambda i, j: (i * tiles_per_b + j, 0, 0)
    return pl.pallas_call(
        functools.partial(_post_kernel, alpha),
        grid=(b, tiles_per_b),
        in_specs=[
            pl.BlockSpec((1, t, d), lambda i, j: (i, j, 0)),
            pl.BlockSpec((1, t, g.shape[2]), lambda i, j: (i, j, 0)),
            pl.BlockSpec((1, N_HEADS, HEAD_DIM, t), lambda i, j: (i, 0, 0, j)),
            pl.BlockSpec((1, 6, d), lambda i, j: (i, 0, 0)),
            full2(w_g), full2(w_a), full2(w_b), full2(w_o), full2(ln_g), full2(ln_b), full2(w_rt), full2(b_r), full2(su),
        ],
        out_specs=[
            pl.BlockSpec((1, t, d), lambda i, j: (i, j, 0)),
            pl.BlockSpec((1, t, d), lambda i, j: (i, j, 0)),
            pl.BlockSpec((1, 8, t), tile_idx),
            pl.BlockSpec((1, 8, t), tile_idx),
            pl.BlockSpec((CLASS_ROWS, 128), lambda i, j: (0, 0)),
        ],
        out_shape=[
            jax.ShapeDtypeStruct((b, s, d), F32),
            jax.ShapeDtypeStruct((b, s, d), F32),
            jax.ShapeDtypeStruct((nt, 8, t), jnp.int32),
            jax.ShapeDtypeStruct((nt, 8, t), F32),
            jax.ShapeDtypeStruct((CLASS_ROWS, 128), F32),
        ],
        scratch_shapes=[pltpu.VMEM((CLASS_ROWS, 128), F32)],
        compiler_params=_cparams(("arbitrary", "arbitrary")),
        name="post",
    )(x, g, o, mod, w_g, w_a, w_b, w_o, ln_g, ln_b, w_rt, b_r, su)


def _tile_row(ref, row):
    return ref.at[lax.shift_right_logical(row, 3), pl.ds(row & (SUBLANES - 1), 1)]


def _dispatch_kernel(last_ref, used_ref, nu_ref, dest_ref, u2_ref, xs_ref, zero_buf, sem, zsem):
    t = u2_ref.shape[0] * SUBLANES
    tm8 = zero_buf.shape[0]

    @pl.when(pl.program_id(0) == 0)
    def _():
        zero_buf[...] = jnp.zeros_like(zero_buf)

        def zero_block(block):
            cp = pltpu.make_async_copy(zero_buf, xs_ref.at[pl.ds(block * tm8, tm8)], zsem)
            cp.start()
            cp.wait()

        for e in range(N_CLASSES):
            @pl.when(used_ref[e] > 0)
            def _():
                zero_block(last_ref[e])

        def tail(b, c):
            zero_block(b)
            return c

        lax.fori_loop(nu_ref[0], xs_ref.shape[0] // tm8, tail, 0)

    def issue(g, c):
        for u in range(SUBLANES):
            dst = _tile_row(xs_ref, dest_ref[0, 0, g * SUBLANES + u])
            pltpu.make_async_copy(u2_ref.at[g, pl.ds(u, 1)], dst, sem).start(priority=u % 2)
        return c

    lax.fori_loop(0, t // SUBLANES, issue, 0)
    pltpu.make_async_copy(u2_ref, xs_ref.at[pl.ds(0, t // SUBLANES)], sem).wait()


def _dispatch(last_block, seg_used, n_used, dest, u2, p_rows):
    n8, _, d = u2.shape
    nt = dest.shape[0]
    t = n8 * SUBLANES // nt
    grid_spec = pltpu.PrefetchScalarGridSpec(
        num_scalar_prefetch=3,
        grid=(nt,),
        in_specs=[
            pl.BlockSpec((1, 1, t), lambda i, lb, us, nu: (i, 0, 0), memory_space=pltpu.SMEM),
            pl.BlockSpec((t // SUBLANES, SUBLANES, d), lambda i, lb, us, nu: (i, 0, 0)),
        ],
        out_specs=pl.BlockSpec(memory_space=pl.ANY),
        scratch_shapes=[pltpu.VMEM((TM // SUBLANES, SUBLANES, d), F32), pltpu.SemaphoreType.DMA(()),
                        pltpu.SemaphoreType.DMA(())],
    )
    return pl.pallas_call(
        _dispatch_kernel,
        grid_spec=grid_spec,
        out_shape=jax.ShapeDtypeStruct((p_rows // SUBLANES, SUBLANES, d), F32),
        compiler_params=_cparams(("arbitrary",)),
        name="dispatch",
    )(last_block, seg_used, n_used, dest, u2)


def _expert_kernel(lo_ref, hi_ref, nu_ref, xs_ref, wg1_ref, wu1_ref, wd1_ref, wg2_ref, wu2_ref, wd2_ref, y_ref):
    i = pl.program_id(0)
    d = xs_ref.shape[1]

    @pl.when(i < nu_ref[0])
    def _():
        xb = xs_ref[...].astype(BF16)
        for col, (wg_ref, wu_ref, wd_ref) in enumerate(((wg1_ref, wu1_ref, wd1_ref), (wg2_ref, wu2_ref, wd2_ref))):
            hg = _dot(xb, wg_ref[0])
            hu = _dot(xb, wu_ref[0])
            h = (hg * jax.nn.sigmoid(hg) * hu).astype(BF16)
            y_ref[:, col * d:(col + 1) * d] = _dot(h, wd_ref[0])

    @pl.when(i >= nu_ref[0])
    def _():
        y_ref[...] = jnp.zeros_like(y_ref)


def _experts(block_lo, block_hi, n_used, xs, w_gate, w_up, w_down):
    p, d = xs.shape
    ff = w_gate.shape[2]
    nb = p // TM
    w_in_spec = lambda which: pl.BlockSpec((1, d, ff), lambda i, lo, hi, nu: ((lo, hi)[which][i], 0, 0))
    w_out_spec = lambda which: pl.BlockSpec((1, ff, d), lambda i, lo, hi, nu: ((lo, hi)[which][i], 0, 0))
    grid_spec = pltpu.PrefetchScalarGridSpec(
        num_scalar_prefetch=3,
        grid=(nb,),
        in_specs=[
            pl.BlockSpec((TM, d), lambda i, lo, hi, nu: (jnp.minimum(i, nu[0] - 1), 0)),
            w_in_spec(0), w_in_spec(0), w_out_spec(0), w_in_spec(1), w_in_spec(1), w_out_spec(1),
        ],
        out_specs=pl.BlockSpec((TM, TOP_K * d), lambda i, lo, hi, nu: (i, 0)),
    )
    return pl.pallas_call(
        _expert_kernel,
        grid_spec=grid_spec,
        out_shape=jax.ShapeDtypeStruct((p, TOP_K * d), F32),
        compiler_params=_cparams(("arbitrary",)),
        name="experts",
    )(block_lo, block_hi, n_used, xs, w_gate, w_up, w_down, w_gate, w_up, w_down)


def _combine_kernel(alpha, dest_ref, dnext_ref, w_ref, x1_ref, mod_ref, lng_ref, lnb_ref, y_ref, o_ref, ybuf, sems):
    i = pl.program_id(0)
    t = x1_ref.shape[0]
    slot = lax.rem(i, 2)

    def gather(dref, sl):
        def issue(g, c):
            for u in range(SUBLANES):
                src = _tile_row(y_ref, dref[0, 0, g * SUBLANES + u])
                pltpu.make_async_copy(src, ybuf.at[sl, g, pl.ds(u, 1)], sems.at[sl]).start(priority=u % 2)
            return c

        lax.fori_loop(0, t // SUBLANES, issue, 0)

    @pl.when(i == 0)
    def _():
        gather(dest_ref, 0)

    @pl.when(i + 1 < pl.num_programs(0))
    def _():
        gather(dnext_ref, 1 - slot)

    pltpu.make_async_copy(y_ref.at[pl.ds(0, t // SUBLANES)], ybuf.at[slot], sems.at[slot]).wait()

    g2 = mod_ref[0, 5:6, :]
    d = x1_ref.shape[1]
    yy = ybuf[slot].reshape(t, TOP_K * d)
    ffn = w_ref[:, 0:1] * yy[:, 0:d] + w_ref[:, 1:2] * yy[:, d:2 * d]
    o_ref[...] = _layer_norm(alpha * x1_ref[...] + g2 * ffn, lng_ref[...], lnb_ref[...])


def _combine(alpha, dest, w, x1, mod, ln_g, ln_b, y, tiles_per_b):
    n, d = x1.shape
    nt = dest.shape[0]
    t = n // nt
    return pl.pallas_call(
        functools.partial(_combine_kernel, alpha),
        grid=(nt,),
        in_specs=[
            pl.BlockSpec((1, 1, t), lambda i: (i, 0, 0), memory_space=pltpu.SMEM),
            pl.BlockSpec((1, 1, t), lambda i: (jnp.minimum(i + 1, nt - 1), 0, 0), memory_space=pltpu.SMEM),
            pl.BlockSpec((t, 2), lambda i: (i, 0)),
            pl.BlockSpec((t, d), lambda i: (i, 0)),
            pl.BlockSpec((1, 6, d), lambda i: (i // tiles_per_b, 0, 0)),
            pl.BlockSpec(ln_g.shape, lambda i: (0, 0)),
            pl.BlockSpec(ln_b.shape, lambda i: (0, 0)),
            pl.BlockSpec(memory_space=pl.ANY),
        ],
        out_specs=pl.BlockSpec((t, d), lambda i: (i, 0)),
        out_shape=jax.ShapeDtypeStruct((n, d), F32),
        scratch_shapes=[pltpu.VMEM((2, t // SUBLANES, SUBLANES, TOP_K * d), F32), pltpu.SemaphoreType.DMA((2,))],
        compiler_params=_cparams(("arbitrary",)),
        name="combine",
    )(dest, dest, w, x1, mod, ln_g, ln_b, y)


def kernel(x, c, w_ada, b_ada, w_in, w_conv, w_a, w_b, w_o, ln1_g, ln1_b, w_router, b_router, w_gate, w_up, w_down,
           ln2_g, ln2_b):
    bsz, seq, d = x.shape
    depth = w_ada.shape[0]
    n = bsz * seq
    cch = w_conv.shape[2]
    alpha = float((2 * depth) ** 0.25)
    n_conv = 3 * cch
    n_qkv = 3 * ATTN_W

    c_pad = jnp.zeros((8, d), F32).at[:bsz].set(c)
    mod_all = _ada(c_pad, w_ada, b_ada)

    tk = min(TK, seq)
    ii = jnp.arange(tk, dtype=jnp.int32)
    trit = (ii[None, :] > ii[:, None]).astype(BF16)
    t_post = min(T_POST, seq)
    jj = jnp.arange(t_post, dtype=jnp.int32)
    su = (jj[:, None] < jj[None, :]).astype(BF16)
    w_rt = w_router.T
    b_r = b_router.reshape(N_EXPERTS, 1)

    n_blocks = n // TM + N_CLASSES
    p_rows = n_blocks * TM
    tiles_per_b = seq // t_post
    c_ids = jnp.arange(N_CLASSES, dtype=jnp.int32)
    pair_of = c_ids % len(PAIR_LO)
    cls_lo = (c_ids // len(PAIR_LO)) * EXPERTS_PER_GROUP + jnp.asarray(PAIR_LO, jnp.int32)[pair_of]
    cls_hi = (c_ids // len(PAIR_LO)) * EXPERTS_PER_GROUP + jnp.asarray(PAIR_HI, jnp.int32)[pair_of]

    for l in range(depth):
        mod = mod_all[l, :bsz].reshape(bsz, 6, d)
        w_in_l = w_in[l].astype(BF16)
        n_qk = 2 * ATTN_W
        g, q, k, vt = _inproj(x, mod, w_in_l[:, :n_conv], w_in_l[:, n_conv:n_conv + n_qk],
                              w_in_l[:, n_conv + n_qk:n_conv + n_qkv].T, w_conv[l], tk)
        o = _attention(q, k, vt, trit)
        x1, u2, ri, rw, cnt = _post(
            alpha, x, g, o, mod, w_in_l[:, n_conv + n_qkv:], w_a[l].astype(BF16), w_b[l].astype(BF16),
            w_o[l].astype(BF16), ln1_g[l].reshape(1, d), ln1_b[l].reshape(1, d), w_rt, b_r, su)

        counts = cnt[:N_CLASSES, 0].astype(jnp.int32)
        pad_counts = (counts + TM - 1) // TM * TM
        pad_end = jnp.cumsum(pad_counts)
        pad_start = pad_end - pad_counts
        cls = ri[:, 0:1, :]
        seg_start = jnp.sum(jnp.where(cls[..., None] == c_ids, pad_start, 0), axis=-1)
        dest = seg_start + ri[:, 1:2, :]
        block_start = jnp.arange(n_blocks, dtype=jnp.int32) * TM
        block_cls = jnp.minimum(jnp.sum((block_start[:, None] >= pad_end[None, :]).astype(jnp.int32), axis=1),
                                N_CLASSES - 1)
        block_lo = jnp.sum(jnp.where(block_cls[:, None] == c_ids, cls_lo, 0), axis=1)
        block_hi = jnp.sum(jnp.where(block_cls[:, None] == c_ids, cls_hi, 0), axis=1)
        n_used = (pad_end[N_CLASSES - 1] // TM).astype(jnp.int32).reshape(1)
        gate_w = jnp.transpose(rw[:, 0:2, :], (0, 2, 1)).reshape(n, 2)

        xs = _dispatch(pad_end // TM - 1, pad_counts, n_used, dest, u2.reshape(n // SUBLANES, SUBLANES, d), p_rows)
        y = _experts(block_lo, block_hi, n_used, xs.reshape(p_rows, d), w_gate[l].astype(BF16), w_up[l].astype(BF16),
                     w_down[l].astype(BF16))
        x = _combine(alpha, dest, gate_w, x1.reshape(n, d), mod, ln2_g[l].reshape(1, d), ln2_b[l].reshape(1, d),
                     y.reshape(p_rows // SUBLANES, SUBLANES, TOP_K * d), tiles_per_b).reshape(bsz, seq, d)
    return x
```

```python
import functools

import jax
import jax.numpy as jnp
from jax import lax
from jax.experimental import pallas as pl
from jax.experimental.pallas import tpu as pltpu

F32 = jnp.float32
BF16 = jnp.bfloat16

N_HEADS = 8
HEAD_DIM = 64
ATTN_W = N_HEADS * HEAD_DIM
N_EXPERTS = 16
N_GROUPS = 4
EXPERTS_PER_GROUP = N_EXPERTS // N_GROUPS
TOP_K = 2
PAIR_LO = (0, 0, 0, 1, 1, 2)
PAIR_HI = (1, 2, 3, 2, 3, 3)
N_CLASSES = N_GROUPS * len(PAIR_LO)
CLASS_ROWS = 32
CONV_K = 3
LN_EPS = 1e-5

T_INPROJ = 512
T_POST = 512
TK = 256
ATTN_HEADS_PER_STEP = 4
N_NEAR = 3
TM = 512
SUBLANES = 8
ADA_TN = 1536

R_STOP = 110.0

VMEM_LIMIT = 56 * 1024 * 1024
ATTN_FLAGS = None


def _cparams(sem, flags=None):
    return pltpu.CompilerParams(dimension_semantics=sem, vmem_limit_bytes=VMEM_LIMIT, flags=flags)


def _dot(a, b):
    return jnp.dot(a, b, preferred_element_type=F32)


def _dot_nt(a, b):
    return lax.dot_general(a, b, (((1,), (1,)), ((), ())), preferred_element_type=F32)


def _split_bf16(v):
    hi = v.astype(BF16)
    lo = (v - hi.astype(F32)).astype(BF16)
    return hi, lo


def _ada_kernel(c_ref, w_ref, b_ref, o_ref):
    c = c_ref[...]
    s = c * jax.nn.sigmoid(c)
    o_ref[0] = jnp.dot(s, w_ref[0], precision=lax.Precision.HIGHEST, preferred_element_type=F32) + b_ref[0]


def _ada(c_pad, w_ada, b_ada):
    depth, d, n6 = w_ada.shape
    rows = c_pad.shape[0]
    return pl.pallas_call(
        _ada_kernel,
        grid=(depth, n6 // ADA_TN),
        in_specs=[
            pl.BlockSpec((rows, d), lambda l, j: (0, 0)),
            pl.BlockSpec((1, d, ADA_TN), lambda l, j: (l, 0, j)),
            pl.BlockSpec((1, 1, ADA_TN), lambda l, j: (l, 0, j)),
        ],
        out_specs=pl.BlockSpec((1, rows, ADA_TN), lambda l, j: (l, 0, j)),
        out_shape=jax.ShapeDtypeStruct((depth, rows, n6), F32),
        compiler_params=_cparams(("arbitrary", "arbitrary")),
        name="adaln",
    )(c_pad, w_ada, b_ada.reshape(depth, 1, n6))


def _inproj_kernel(x_ref, mod_ref, wc_ref, wqk_ref, wvt_ref, wconv_ref, g_ref, q_ref, k_ref, vt_ref, carry_ref):
    j = pl.program_id(1)
    t = x_ref.shape[1]
    cch = wconv_ref.shape[1]
    tk = vt_ref.shape[4]

    @pl.when(j == 0)
    def _():
        carry_ref[...] = jnp.zeros_like(carry_ref)

    sh1 = mod_ref[0, 0:1, :]
    sc1 = mod_ref[0, 1:2, :]
    u = (x_ref[0] * (1.0 + sc1) + sh1).astype(BF16)

    pc = _dot(u, wc_ref[...])
    cb = pc[:, 0:cch]
    h = pc[:, cch:2 * cch] * pc[:, 2 * cch:3 * cch]
    prev = carry_ref[...]
    p1 = prev[7:8, :]
    p2 = prev[6:7, :]
    row = lax.broadcasted_iota(jnp.int32, h.shape, 0)
    hm1 = jnp.where(row == 0, p1, pltpu.roll(h, 1, 0))
    hm2 = jnp.where(row == 0, p2, jnp.where(row == 1, p1, pltpu.roll(h, 2, 0)))
    wcv = wconv_ref[...]
    conv = wcv[2:3, :] * h + wcv[1:2, :] * hm1 + wcv[0:1, :] * hm2
    g_ref[0] = (cb * conv).astype(BF16)
    carry_ref[...] = h[t - 8:t, :]

    qk = _dot(u, wqk_ref[...])
    scale = HEAD_DIM ** -0.5
    for hd in range(N_HEADS):
        lo = hd * HEAD_DIM
        q_ref[0, hd] = (qk[:, lo:lo + HEAD_DIM] * scale).astype(BF16)
        k_ref[0, hd] = qk[:, ATTN_W + lo:ATTN_W + lo + HEAD_DIM].astype(BF16)
    vt = _dot_nt(wvt_ref[...], u)
    for hd in range(N_HEADS):
        for cj in range(t // tk):
            vt_ref[0, hd, cj] = vt[hd * HEAD_DIM:(hd + 1) * HEAD_DIM, cj * tk:(cj + 1) * tk].astype(BF16)


def _inproj(x, mod, w_c, w_qk, w_vt, w_conv, tk):
    b, s, d = x.shape
    cch = w_conv.shape[1]
    t = min(T_INPROJ, s)
    hshape = jax.ShapeDtypeStruct((b, N_HEADS, s, HEAD_DIM), BF16)
    hspec = pl.BlockSpec((1, N_HEADS, t, HEAD_DIM), lambda i, j: (i, 0, j, 0))
    return pl.pallas_call(
        _inproj_kernel,
        grid=(b, s // t),
        in_specs=[
            pl.BlockSpec((1, t, d), lambda i, j: (i, j, 0)),
            pl.BlockSpec((1, 6, d), lambda i, j: (i, 0, 0)),
            pl.BlockSpec(w_c.shape, lambda i, j: (0, 0)),
            pl.BlockSpec(w_qk.shape, lambda i, j: (0, 0)),
            pl.BlockSpec(w_vt.shape, lambda i, j: (0, 0)),
            pl.BlockSpec(w_conv.shape, lambda i, j: (0, 0)),
        ],
        out_specs=[
            pl.BlockSpec((1, t, cch), lambda i, j: (i, j, 0)), hspec, hspec,
            pl.BlockSpec((1, N_HEADS, t // tk, HEAD_DIM, tk), lambda i, j: (i, 0, j, 0, 0)),
        ],
        out_shape=[
            jax.ShapeDtypeStruct((b, s, cch), BF16), hshape, hshape,
            jax.ShapeDtypeStruct((b, N_HEADS, s // tk, HEAD_DIM, tk), BF16),
        ],
        scratch_shapes=[pltpu.VMEM((8, cch), F32)],
        compiler_params=_cparams(("arbitrary", "arbitrary")),
        name="inproj",
    )(x, mod, w_c, w_qk, w_vt, w_conv)


def _neg_abs(z):
    bits = lax.bitcast_convert_type(z, jnp.uint32) | jnp.uint32(0x80000000)
    return lax.bitcast_convert_type(bits, F32)


def _attn_kernel(q_ref, k_ref, vt_ref, trit_ref, ot_ref, za_buf, zb_buf, h_buf, r_buf, acc_buf, sat_ref):
    qi = pl.program_id(2)
    tk = za_buf.shape[2]
    nhead = za_buf.shape[1]
    rows = lax.broadcasted_iota(jnp.int32, (tk, tk), 0)
    cols = lax.broadcasted_iota(jnp.int32, (tk, tk), 1)
    causal = rows < cols

    chunk = lambda dist: jnp.maximum(qi - dist, 0)
    all_heads = tuple(range(nhead))

    def step(sc, sp, sw):
        cs = {}
        if sw is not None:
            dw, heads_w, slotw = sw
            for s in heads_w:
                cs[s] = _dot(trit_ref[...], h_buf[slotw, s])
        if sc is not None:
            dc, heads_c, slotc = sc
            k0 = pl.multiple_of(chunk(dc) * tk, tk)
            for s in heads_c:
                za_buf[slotc, s] = _dot_nt(k_ref[0, s, pl.ds(k0, tk), :], q_ref[0, s])
        if sp is not None:
            dp, heads_p, slotp = sp
            for s in heads_p:
                zt = za_buf[slotp, s]
                v = jnp.maximum(zt, 0.0) + jnp.log(1.0 + jnp.exp(_neg_abs(zt)))
                zb_buf[slotp, s] = zt - v
                if isinstance(dp, int) and dp == 0:
                    v = jnp.where(causal, v, 0.0)
                h_buf[slotp, s] = v.astype(BF16)
        if sw is not None:
            cw = chunk(dw)
            on = None if isinstance(dw, int) and dw == 0 else (dw <= qi).astype(F32)
            for s in heads_w:
                r = r_buf[s]
                p = jnp.exp(zb_buf[slotw, s] - cs[s])
                if on is None:
                    p = jnp.where(causal, p, 0.0)
                scale = jnp.exp(-r)
                add = cs[s][0:1, :] + h_buf[slotw, s, 0:1, :].astype(F32)
                if on is not None:
                    scale = scale * on
                    add = add * on
                acc_buf[s] += _dot(vt_ref[0, s, cw], p.astype(BF16)) * scale
                r_buf[s] = r + add

    def saturated():
        return (jnp.min(r_buf[...]) >= R_STOP).astype(jnp.int32)

    r_buf[...] = jnp.zeros_like(r_buf)
    acc_buf[...] = jnp.zeros_like(acc_buf)

    items = [(d, (s,), d) for d in range(N_NEAR) for s in range(nhead)]
    at = lambda m: items[m] if 0 <= m < len(items) else None
    for m in range(len(items) + 2):
        step(at(m), at(m - 1), at(m - 2))

    sat_ref[0] = saturated()

    @pl.when(jnp.logical_and(qi >= N_NEAR, sat_ref[0] == 0))
    def _():
        p0, p1 = N_NEAR % 2, (N_NEAR + 1) % 2
        step((N_NEAR, all_heads, p0), None, None)
        step((N_NEAR + 1, all_heads, p1), (N_NEAR, all_heads, p0), None)

        def not_done(carry):
            j, done = carry
            return jnp.logical_and(N_NEAR + 2 * j <= qi, done == 0)

        def pair(carry):
            j, _ = carry
            d = N_NEAR + 2 * j
            step((d + 2, all_heads, p0), (d + 1, all_heads, p1), (d, all_heads, p0))
            sat_ref[0] = saturated()

            @pl.when(jnp.logical_and(d + 1 <= qi, sat_ref[0] == 0))
            def _():
                step((d + 3, all_heads, p1), (d + 2, all_heads, p0), (d + 1, all_heads, p1))
                sat_ref[0] = saturated()

            return j + 1, sat_ref[0]

        lax.while_loop(not_done, pair, (jnp.int32(0), jnp.int32(0)))

    for s in range(nhead):
        ot_ref[0, s] = acc_buf[s].astype(BF16)


def _attention(q, k, vt, trit):
    b, nh, s, hd = q.shape
    tk = trit.shape[0]
    g = ATTN_HEADS_PER_STEP
    return pl.pallas_call(
        _attn_kernel,
        grid=(b, nh // g, s // tk),
        in_specs=[
            pl.BlockSpec((1, g, tk, hd), lambda i, h, j: (i, h, j, 0)),
            pl.BlockSpec((1, g, s, hd), lambda i, h, j: (i, h, 0, 0)),
            pl.BlockSpec((1, g, s // tk, hd, tk), lambda i, h, j: (i, h, 0, 0, 0)),
            pl.BlockSpec(trit.shape, lambda i, h, j: (0, 0)),
        ],
        out_specs=pl.BlockSpec((1, g, hd, tk), lambda i, h, j: (i, h, 0, j)),
        out_shape=jax.ShapeDtypeStruct((b, nh, hd, s), BF16),
        scratch_shapes=[
            pltpu.VMEM((N_NEAR, g, tk, tk), F32),
            pltpu.VMEM((N_NEAR, g, tk, tk), F32),
            pltpu.VMEM((N_NEAR, g, tk, tk), BF16),
            pltpu.VMEM((g, 1, tk), F32),
            pltpu.VMEM((g, hd, tk), F32),
            pltpu.SMEM((1,), jnp.int32),
        ],
        compiler_params=_cparams(("arbitrary", "arbitrary", "arbitrary"), ATTN_FLAGS),
        name="attn",
    )(q, k, vt, trit)


def _layer_norm(h, g, b):
    mu = jnp.mean(h, axis=-1, keepdims=True)
    hc = h - mu
    var = jnp.mean(hc * hc, axis=-1, keepdims=True)
    return hc * lax.rsqrt(var + LN_EPS) * g + b


def _first_argmax(vals):
    best = vals[0]
    idx = jnp.zeros(best.shape, jnp.int32)
    for j in range(1, len(vals)):
        upd = vals[j] > best
        best = jnp.where(upd, vals[j], best)
        idx = jnp.where(upd, j, idx)
    return best, idx


def _pick(idx, vals):
    out = vals[0]
    for j in range(1, len(vals)):
        out = jnp.where(idx == j, vals[j], out)
    return out


def _post_kernel(alpha, x_ref, g_ref, o_ref, mod_ref, wg_ref, wa_ref, wb_ref, wo_ref, lng_ref, lnb_ref,
                 wrt_ref, br_ref, su_ref,
                 x1_ref, u2_ref, ri_ref, rw_ref, cnt_ref, carry_ref):
    first = jnp.logical_and(pl.program_id(0) == 0, pl.program_id(1) == 0)

    @pl.when(first)
    def _():
        carry_ref[...] = jnp.zeros_like(carry_ref)

    d = x_ref.shape[2]
    x = x_ref[0]
    sh1 = mod_ref[0, 0:1, :]
    sc1 = mod_ref[0, 1:2, :]
    g1 = mod_ref[0, 2:3, :]
    sh2 = mod_ref[0, 3:4, :]
    sc2 = mod_ref[0, 4:5, :]
    u = (x * (1.0 + sc1) + sh1).astype(BF16)
    gates = _dot(u, wg_ref[...])
    y_a = _dot(g_ref[0], wa_ref[...])
    o_t = o_ref[0].reshape(ATTN_W, x.shape[0])
    y_b = lax.dot_general(o_t, wb_ref[...], (((0,), (0,)), ((), ())), preferred_element_type=F32)
    merged = jax.nn.sigmoid(gates[:, 0:d]) * y_a + jax.nn.sigmoid(gates[:, d:2 * d]) * y_b
    mix = _dot(merged.astype(BF16), wo_ref[...])
    x1 = _layer_norm(alpha * x + g1 * mix, lng_ref[...], lnb_ref[...])
    x1_ref[0] = x1
    u2 = x1 * (1.0 + sc2) + sh2
    u2_ref[0] = u2

    uh, ul = _split_bf16(u2)
    wh, wl = _split_bf16(wrt_ref[...])
    logits = _dot_nt(wh, uh) + _dot_nt(wh, ul) + _dot_nt(wl, uh)
    aff = jax.nn.sigmoid(logits)
    sel = aff + br_ref[...]
    selr = [sel[e:e + 1, :] for e in range(N_EXPERTS)]
    affr = [aff[e:e + 1, :] for e in range(N_EXPERTS)]
    gscore = []
    for gi in range(N_GROUPS):
        m = selr[gi * EXPERTS_PER_GROUP:(gi + 1) * EXPERTS_PER_GROUP]
        best = None
        for a in range(EXPERTS_PER_GROUP):
            for bb in range(a + 1, EXPERTS_PER_GROUP):
                pair = m[a] + m[bb]
                best = pair if best is None else jnp.maximum(best, pair)
        gscore.append(best)
    _, gidx = _first_argmax(gscore)
    sel_in = [_pick(gidx, [selr[gi * EXPERTS_PER_GROUP + j] for gi in range(N_GROUPS)]) for j in range(EXPERTS_PER_GROUP)]
    aff_in = [_pick(gidx, [affr[gi * EXPERTS_PER_GROUP + j] for gi in range(N_GROUPS)]) for j in range(EXPERTS_PER_GROUP)]
    _, i1 = _first_argmax(sel_in)
    neg = jnp.full_like(sel_in[0], -jnp.inf)
    _, i2 = _first_argmax([jnp.where(i1 == j, neg, sel_in[j]) for j in range(EXPERTS_PER_GROUP)])
    a1 = _pick(i1, aff_in)
    a2 = _pick(i2, aff_in)
    den = a1 + a2

    lo = jnp.minimum(i1, i2)
    hi = jnp.maximum(i1, i2)
    pair = jnp.where(lo == 0, hi - 1, jnp.where(lo == 1, hi + 1, len(PAIR_LO) - 1))
    cls = gidx * len(PAIR_LO) + pair
    w1 = a1 / den
    w2 = a2 / den
    first_is_lo = i1 < i2
    w_lo = jnp.where(first_is_lo, w1, w2)
    w_hi = jnp.where(first_is_lo, w2, w1)

    cio = lax.broadcasted_iota(jnp.int32, (CLASS_ROWS, cls.shape[1]), 0)
    hit = cio == cls
    member = jnp.where(hit, 1.0, 0.0)
    before = _dot(member.astype(BF16), su_ref[...]) + carry_ref[:, 0:1]
    rank = jnp.sum(jnp.where(hit, before, 0.0), axis=0, keepdims=True)
    t = member.shape[1]
    total = before[:, t - 1:t] + member[:, t - 1:t]
    carry_ref[...] = jnp.broadcast_to(total, carry_ref.shape)
    cnt_ref[...] = jnp.broadcast_to(total, cnt_ref.shape)

    zi = jnp.zeros_like(cls)
    ri_ref[0] = jnp.concatenate([cls, rank.astype(jnp.int32), zi, zi, zi, zi, zi, zi], axis=0)
    zf = jnp.zeros_like(w_lo)
    rw_ref[0] = jnp.concatenate([w_lo, w_hi, zf, zf, zf, zf, zf, zf], axis=0)


def _post(alpha, x, g, o, mod, w_g, w_a, w_b, w_o, ln_g, ln_b, w_rt, b_r, su):
    b, s, d = x.shape
    t = su.shape[0]
    nt = (b * s) // t
    tiles_per_b = s // t
    full2 = lambda a: pl.BlockSpec(a.shape, lambda i, j: (0, 0))
    tile_idx = lambda i, j: (i * tiles_per_b + j, 0, 0)
    return pl.pallas_call(
        functools.partial(_post_kernel, alpha),
        grid=(b, tiles_per_b),
        in_specs=[
            pl.BlockSpec((1, t, d), lambda i, j: (i, j, 0)),
            pl.BlockSpec((1, t, g.shape[2]), lambda i, j: (i, j, 0)),
            pl.BlockSpec((1, N_HEADS, HEAD_DIM, t), lambda i, j: (i, 0, 0, j)),
            pl.BlockSpec((1, 6, d), lambda i, j: (i, 0, 0)),
            full2(w_g), full2(w_a), full2(w_b), full2(w_o), full2(ln_g), full2(ln_b), full2(w_rt), full2(b_r), full2(su),
        ],
        out_specs=[
            pl.BlockSpec((1, t, d), lambda i, j: (i, j, 0)),
            pl.BlockSpec((1, t, d), lambda i, j: (i, j, 0)),
            pl.BlockSpec((1, 8, t), tile_idx),
            pl.BlockSpec((1, 8, t), tile_idx),
            pl.BlockSpec((CLASS_ROWS, 128), lambda i, j: (0, 0)),
        ],
        out_shape=[
            jax.ShapeDtypeStruct((b, s, d), F32),
            jax.ShapeDtypeStruct((b, s, d), F32),
            jax.ShapeDtypeStruct((nt, 8, t), jnp.int32),
            jax.ShapeDtypeStruct((nt, 8, t), F32),
            jax.ShapeDtypeStruct((CLASS_ROWS, 128), F32),
        ],
        scratch_shapes=[pltpu.VMEM((CLASS_ROWS, 128), F32)],
        compiler_params=_cparams(("arbitrary", "arbitrary")),
        name="post",
    )(x, g, o, mod, w_g, w_a, w_b, w_o, ln_g, ln_b, w_rt, b_r, su)


def _tile_row(ref, row):
    return ref.at[lax.shift_right_logical(row, 3), pl.ds(row & (SUBLANES - 1), 1)]


def _dispatch_kernel(last_ref, used_ref, nu_ref, dest_ref, u2_ref, xs_ref, zero_buf, sem, zsem):
    t = u2_ref.shape[0] * SUBLANES
    tm8 = zero_buf.shape[0]

    @pl.when(pl.program_id(0) == 0)
    def _():
        zero_buf[...] = jnp.zeros_like(zero_buf)

        def zero_block(block):
            cp = pltpu.make_async_copy(zero_buf, xs_ref.at[pl.ds(block * tm8, tm8)], zsem)
            cp.start()
            cp.wait()

        for e in range(N_CLASSES):
            @pl.when(used_ref[e] > 0)
            def _():
                zero_block(last_ref[e])

        def tail(b, c):
            zero_block(b)
            return c

        lax.fori_loop(nu_ref[0], xs_ref.shape[0] // tm8, tail, 0)

    def issue(g, c):
        for u in range(SUBLANES):
            dst = _tile_row(xs_ref, dest_ref[0, 0, g * SUBLANES + u])
            pltpu.make_async_copy(u2_ref.at[g, pl.ds(u, 1)], dst, sem).start(priority=u % 2)
        return c

    lax.fori_loop(0, t // SUBLANES, issue, 0)
    pltpu.make_async_copy(u2_ref, xs_ref.at[pl.ds(0, t // SUBLANES)], sem).wait()


def _dispatch(last_block, seg_used, n_used, dest, u2, p_rows):
    n8, _, d = u2.shape
    nt = dest.shape[0]
    t = n8 * SUBLANES // nt
    grid_spec = pltpu.PrefetchScalarGridSpec(
        num_scalar_prefetch=3,
        grid=(nt,),
        in_specs=[
            pl.BlockSpec((1, 1, t), lambda i, lb, us, nu: (i, 0, 0), memory_space=pltpu.SMEM),
            pl.BlockSpec((t // SUBLANES, SUBLANES, d), lambda i, lb, us, nu: (i, 0, 0)),
        ],
        out_specs=pl.BlockSpec(memory_space=pl.ANY),
        scratch_shapes=[pltpu.VMEM((TM // SUBLANES, SUBLANES, d), F32), pltpu.SemaphoreType.DMA(()),
                        pltpu.SemaphoreType.DMA(())],
    )
    return pl.pallas_call(
        _dispatch_kernel,
        grid_spec=grid_spec,
        out_shape=jax.ShapeDtypeStruct((p_rows // SUBLANES, SUBLANES, d), F32),
        compiler_params=_cparams(("arbitrary",)),
        name="dispatch",
    )(last_block, seg_used, n_used, dest, u2)


def _expert_kernel(lo_ref, hi_ref, nu_ref, xs_ref, wg1_ref, wu1_ref, wd1_ref, wg2_ref, wu2_ref, wd2_ref, y_ref):
    i = pl.program_id(0)
    d = xs_ref.shape[1]

    @pl.when(i < nu_ref[0])
    def _():
        xb = xs_ref[...].astype(BF16)
        for col, (wg_ref, wu_ref, wd_ref) in enumerate(((wg1_ref, wu1_ref, wd1_ref), (wg2_ref, wu2_ref, wd2_ref))):
            hg = _dot(xb, wg_ref[0])
            hu = _dot(xb, wu_ref[0])
            h = (hg * jax.nn.sigmoid(hg) * hu).astype(BF16)
            y_ref[:, col * d:(col + 1) * d] = _dot(h, wd_ref[0])

    @pl.when(i >= nu_ref[0])
    def _():
        y_ref[...] = jnp.zeros_like(y_ref)


def _experts(block_lo, block_hi, n_used, xs, w_gate, w_up, w_down):
    p, d = xs.shape
    ff = w_gate.shape[2]
    nb = p // TM
    w_in_spec = lambda which: pl.BlockSpec((1, d, ff), lambda i, lo, hi, nu: ((lo, hi)[which][i], 0, 0))
    w_out_spec = lambda which: pl.BlockSpec((1, ff, d), lambda i, lo, hi, nu: ((lo, hi)[which][i], 0, 0))
    grid_spec = pltpu.PrefetchScalarGridSpec(
        num_scalar_prefetch=3,
        grid=(nb,),
        in_specs=[
            pl.BlockSpec((TM, d), lambda i, lo, hi, nu: (jnp.minimum(i, nu[0] - 1), 0)),
            w_in_spec(0), w_in_spec(0), w_out_spec(0), w_in_spec(1), w_in_spec(1), w_out_spec(1),
        ],
        out_specs=pl.BlockSpec((TM, TOP_K * d), lambda i, lo, hi, nu: (i, 0)),
    )
    return pl.pallas_call(
        _expert_kernel,
        grid_spec=grid_spec,
        out_shape=jax.ShapeDtypeStruct((p, TOP_K * d), F32),
        compiler_params=_cparams(("arbitrary",)),
        name="experts",
    )(block_lo, block_hi, n_used, xs, w_gate, w_up, w_down, w_gate, w_up, w_down)


def _combine_kernel(alpha, dest_ref, dnext_ref, w_ref, x1_ref, mod_ref, lng_ref, lnb_ref, y_ref, o_ref, ybuf, sems):
    i = pl.program_id(0)
    t = x1_ref.shape[0]
    slot = lax.rem(i, 2)

    def gather(dref, sl):
        def issue(g, c):
            for u in range(SUBLANES):
                src = _tile_row(y_ref, dref[0, 0, g * SUBLANES + u])
                pltpu.make_async_copy(src, ybuf.at[sl, g, pl.ds(u, 1)], sems.at[sl]).start(priority=u % 2)
            return c

        lax.fori_loop(0, t // SUBLANES, issue, 0)

    @pl.when(i == 0)
    def _():
        gather(dest_ref, 0)

    @pl.when(i + 1 < pl.num_programs(0))
    def _():
        gather(dnext_ref, 1 - slot)

    pltpu.make_async_copy(y_ref.at[pl.ds(0, t // SUBLANES)], ybuf.at[slot], sems.at[slot]).wait()

    g2 = mod_ref[0, 5:6, :]
    d = x1_ref.shape[1]
    yy = ybuf[slot].reshape(t, TOP_K * d)
    ffn = w_ref[:, 0:1] * yy[:, 0:d] + w_ref[:, 1:2] * yy[:, d:2 * d]
    o_ref[...] = _layer_norm(alpha * x1_ref[...] + g2 * ffn, lng_ref[...], lnb_ref[...])


def _combine(alpha, dest, w, x1, mod, ln_g, ln_b, y, tiles_per_b):
    n, d = x1.shape
    nt = dest.shape[0]
    t = n // nt
    return pl.pallas_call(
        functools.partial(_combine_kernel, alpha),
        grid=(nt,),
        in_specs=[
            pl.BlockSpec((1, 1, t), lambda i: (i, 0, 0), memory_space=pltpu.SMEM),
            pl.BlockSpec((1, 1, t), lambda i: (jnp.minimum(i + 1, nt - 1), 0, 0), memory_space=pltpu.SMEM),
            pl.BlockSpec((t, 2), lambda i: (i, 0)),
            pl.BlockSpec((t, d), lambda i: (i, 0)),
            pl.BlockSpec((1, 6, d), lambda i: (i // tiles_per_b, 0, 0)),
            pl.BlockSpec(ln_g.shape, lambda i: (0, 0)),
            pl.BlockSpec(ln_b.shape, lambda i: (0, 0)),
            pl.BlockSpec(memory_space=pl.ANY),
        ],
        out_specs=pl.BlockSpec((t, d), lambda i: (i, 0)),
        out_shape=jax.ShapeDtypeStruct((n, d), F32),
        scratch_shapes=[pltpu.VMEM((2, t // SUBLANES, SUBLANES, TOP_K * d), F32), pltpu.SemaphoreType.DMA((2,))],
        compiler_params=_cparams(("arbitrary",)),
        name="combine",
    )(dest, dest, w, x1, mod, ln_g, ln_b, y)


def kernel(x, c, w_ada, b_ada, w_in, w_conv, w_a, w_b, w_o, ln1_g, ln1_b, w_router, b_router, w_gate, w_up, w_down,
           ln2_g, ln2_b):
    bsz, seq, d = x.shape
    depth = w_ada.shape[0]
    n = bsz * seq
    cch = w_conv.shape[2]
    alpha = float((2 * depth) ** 0.25)
    n_conv = 3 * cch
    n_qkv = 3 * ATTN_W

    c_pad = jnp.zeros((8, d), F32).at[:bsz].set(c)
    mod_all = _ada(c_pad, w_ada, b_ada)

    tk = min(TK, seq)
    ii = jnp.arange(tk, dtype=jnp.int32)
    trit = (ii[None, :] > ii[:, None]).astype(BF16)
    t_post = min(T_POST, seq)
    jj = jnp.arange(t_post, dtype=jnp.int32)
    su = (jj[:, None] < jj[None, :]).astype(BF16)
    w_rt = w_router.T
    b_r = b_router.reshape(N_EXPERTS, 1)

    n_blocks = n // TM + N_CLASSES
    p_rows = n_blocks * TM
    tiles_per_b = seq // t_post
    c_ids = jnp.arange(N_CLASSES, dtype=jnp.int32)
    pair_of = c_ids % len(PAIR_LO)
    cls_lo = (c_ids // len(PAIR_LO)) * EXPERTS_PER_GROUP + jnp.asarray(PAIR_LO, jnp.int32)[pair_of]
    cls_hi = (c_ids // len(PAIR_LO)) * EXPERTS_PER_GROUP + jnp.asarray(PAIR_HI, jnp.int32)[pair_of]

    for l in range(depth):
        mod = mod_all[l, :bsz].reshape(bsz, 6, d)
        w_in_l = w_in[l].astype(BF16)
        n_qk = 2 * ATTN_W
        g, q, k, vt = _inproj(x, mod, w_in_l[:, :n_conv], w_in_l[:, n_conv:n_conv + n_qk],
                              w_in_l[:, n_conv + n_qk:n_conv + n_qkv].T, w_conv[l], tk)
        o = _attention(q, k, vt, trit)
        x1, u2, ri, rw, cnt = _post(
            alpha, x, g, o, mod, w_in_l[:, n_conv + n_qkv:], w_a[l].astype(BF16), w_b[l].astype(BF16),
            w_o[l].astype(BF16), ln1_g[l].reshape(1, d), ln1_b[l].reshape(1, d), w_rt, b_r, su)

        counts = cnt[:N_CLASSES, 0].astype(jnp.int32)
        pad_counts = (counts + TM - 1) // TM * TM
        pad_end = jnp.cumsum(pad_counts)
        pad_start = pad_end - pad_counts
        cls = ri[:, 0:1, :]
        seg_start = jnp.sum(jnp.where(cls[..., None] == c_ids, pad_start, 0), axis=-1)
        dest = seg_start + ri[:, 1:2, :]
        block_start = jnp.arange(n_blocks, dtype=jnp.int32) * TM
        block_cls = jnp.minimum(jnp.sum((block_start[:, None] >= pad_end[None, :]).astype(jnp.int32), axis=1),
                                N_CLASSES - 1)
        block_lo = jnp.sum(jnp.where(block_cls[:, None] == c_ids, cls_lo, 0), axis=1)
        block_hi = jnp.sum(jnp.where(block_cls[:, None] == c_ids, cls_hi, 0), axis=1)
        n_used = (pad_end[N_CLASSES - 1] // TM).astype(jnp.int32).reshape(1)
        gate_w = jnp.transpose(rw[:, 0:2, :], (0, 2, 1)).reshape(n, 2)

        xs = _dispatch(pad_end // TM - 1, pad_counts, n_used, dest, u2.reshape(n // SUBLANES, SUBLANES, d), p_rows)
        y = _experts(block_lo, block_hi, n_used, xs.reshape(p_rows, d), w_gate[l].astype(BF16), w_up[l].astype(BF16),
                     w_down[l].astype(BF16))
        x = _combine(alpha, dest, gate_w, x1.reshape(n, d), mod, ln2_g[l].reshape(1, d), ln2_b[l].reshape(1, d),
                     y.reshape(p_rows // SUBLANES, SUBLANES, TOP_K * d), tiles_per_b).reshape(bsz, seq, d)
    return x
```

```python
import functools

import jax
import jax.numpy as jnp
from jax import lax
from jax.experimental import pallas as pl
from jax.experimental.pallas import tpu as pltpu

F32 = jnp.float32
BF16 = jnp.bfloat16

N_HEADS = 8
HEAD_DIM = 64
ATTN_W = N_HEADS * HEAD_DIM
N_EXPERTS = 16
N_GROUPS = 4
EXPERTS_PER_GROUP = N_EXPERTS // N_GROUPS
TOP_K = 2
PAIR_LO = (0, 0, 0, 1, 1, 2)
PAIR_HI = (1, 2, 3, 2, 3, 3)
N_CLASSES = N_GROUPS * len(PAIR_LO)
CLASS_ROWS = 32
CONV_K = 3
LN_EPS = 1e-5

T_INPROJ = 512
T_POST = 512
TK = 256
ATTN_HEADS_PER_STEP = 4
ATTN_TILES_PER_STEP = 2
N_NEAR = 3
TM = 512
SUBLANES = 8
ADA_TN = 1536

R_STOP = 110.0

VMEM_LIMIT = 56 * 1024 * 1024
ATTN_FLAGS = None


def _cparams(sem, flags=None):
    return pltpu.CompilerParams(dimension_semantics=sem, vmem_limit_bytes=VMEM_LIMIT, flags=flags)


def _dot(a, b):
    return jnp.dot(a, b, preferred_element_type=F32)


def _dot_nt(a, b):
    return lax.dot_general(a, b, (((1,), (1,)), ((), ())), preferred_element_type=F32)


def _split_bf16(v):
    hi = v.astype(BF16)
    lo = (v - hi.astype(F32)).astype(BF16)
    return hi, lo


def _ada_kernel(c_ref, w_ref, b_ref, o_ref):
    c = c_ref[...]
    s = c * jax.nn.sigmoid(c)
    o_ref[0] = jnp.dot(s, w_ref[0], precision=lax.Precision.HIGHEST, preferred_element_type=F32) + b_ref[0]


def _ada(c_pad, w_ada, b_ada):
    depth, d, n6 = w_ada.shape
    rows = c_pad.shape[0]
    return pl.pallas_call(
        _ada_kernel,
        grid=(depth, n6 // ADA_TN),
        in_specs=[
            pl.BlockSpec((rows, d), lambda l, j: (0, 0)),
            pl.BlockSpec((1, d, ADA_TN), lambda l, j: (l, 0, j)),
            pl.BlockSpec((1, 1, ADA_TN), lambda l, j: (l, 0, j)),
        ],
        out_specs=pl.BlockSpec((1, rows, ADA_TN), lambda l, j: (l, 0, j)),
        out_shape=jax.ShapeDtypeStruct((depth, rows, n6), F32),
        compiler_params=_cparams(("arbitrary", "arbitrary")),
        name="adaln",
    )(c_pad, w_ada, b_ada.reshape(depth, 1, n6))


def _inproj_kernel(x_ref, mod_ref, wc_ref, wqk_ref, wvt_ref, wconv_ref, g_ref, q_ref, k_ref, vt_ref, carry_ref):
    j = pl.program_id(1)
    t = x_ref.shape[1]
    cch = wconv_ref.shape[1]
    tk = vt_ref.shape[4]

    @pl.when(j == 0)
    def _():
        carry_ref[...] = jnp.zeros_like(carry_ref)

    sh1 = mod_ref[0, 0:1, :]
    sc1 = mod_ref[0, 1:2, :]
    u = (x_ref[0] * (1.0 + sc1) + sh1).astype(BF16)

    pc = _dot(u, wc_ref[...])
    cb = pc[:, 0:cch]
    h = pc[:, cch:2 * cch] * pc[:, 2 * cch:3 * cch]
    prev = carry_ref[...]
    p1 = prev[7:8, :]
    p2 = prev[6:7, :]
    row = lax.broadcasted_iota(jnp.int32, h.shape, 0)
    hm1 = jnp.where(row == 0, p1, pltpu.roll(h, 1, 0))
    hm2 = jnp.where(row == 0, p2, jnp.where(row == 1, p1, pltpu.roll(h, 2, 0)))
    wcv = wconv_ref[...]
    conv = wcv[2:3, :] * h + wcv[1:2, :] * hm1 + wcv[0:1, :] * hm2
    g_ref[0] = (cb * conv).astype(BF16)
    carry_ref[...] = h[t - 8:t, :]

    qk = _dot(u, wqk_ref[...])
    scale = HEAD_DIM ** -0.5
    for hd in range(N_HEADS):
        lo = hd * HEAD_DIM
        q_ref[0, hd] = (qk[:, lo:lo + HEAD_DIM] * scale).astype(BF16)
        k_ref[0, hd] = qk[:, ATTN_W + lo:ATTN_W + lo + HEAD_DIM].astype(BF16)
    vt = _dot_nt(wvt_ref[...], u)
    for hd in range(N_HEADS):
        for cj in range(t // tk):
            vt_ref[0, hd, cj] = vt[hd * HEAD_DIM:(hd + 1) * HEAD_DIM, cj * tk:(cj + 1) * tk].astype(BF16)


def _inproj(x, mod, w_c, w_qk, w_vt, w_conv, tk):
    b, s, d = x.shape
    cch = w_conv.shape[1]
    t = min(T_INPROJ, s)
    hshape = jax.ShapeDtypeStruct((b, N_HEADS, s, HEAD_DIM), BF16)
    hspec = pl.BlockSpec((1, N_HEADS, t, HEAD_DIM), lambda i, j: (i, 0, j, 0))
    return pl.pallas_call(
        _inproj_kernel,
        grid=(b, s // t),
        in_specs=[
            pl.BlockSpec((1, t, d), lambda i, j: (i, j, 0)),
            pl.BlockSpec((1, 6, d), lambda i, j: (i, 0, 0)),
            pl.BlockSpec(w_c.shape, lambda i, j: (0, 0)),
            pl.BlockSpec(w_qk.shape, lambda i, j: (0, 0)),
            pl.BlockSpec(w_vt.shape, lambda i, j: (0, 0)),
            pl.BlockSpec(w_conv.shape, lambda i, j: (0, 0)),
        ],
        out_specs=[
            pl.BlockSpec((1, t, cch), lambda i, j: (i, j, 0)), hspec, hspec,
            pl.BlockSpec((1, N_HEADS, t // tk, HEAD_DIM, tk), lambda i, j: (i, 0, j, 0, 0)),
        ],
        out_shape=[
            jax.ShapeDtypeStruct((b, s, cch), BF16), hshape, hshape,
            jax.ShapeDtypeStruct((b, N_HEADS, s // tk, HEAD_DIM, tk), BF16),
        ],
        scratch_shapes=[pltpu.VMEM((8, cch), F32)],
        compiler_params=_cparams(("arbitrary", "arbitrary")),
        name="inproj",
    )(x, mod, w_c, w_qk, w_vt, w_conv)


def _neg_abs(z):
    bits = lax.bitcast_convert_type(z, jnp.uint32) | jnp.uint32(0x80000000)
    return lax.bitcast_convert_type(bits, F32)


def _attn_kernel(q_ref, k_ref, vt_ref, trit_ref, ot_ref, za_buf, zb_buf, h_buf, r_buf, acc_buf, sat_ref):
    tk = za_buf.shape[2]
    nhead = za_buf.shape[1]
    ntile = za_buf.shape[0] // N_NEAR
    tile = lambda tl: pl.program_id(2) * ntile + tl
    rows = lax.broadcasted_iota(jnp.int32, (tk, tk), 0)
    cols = lax.broadcasted_iota(jnp.int32, (tk, tk), 1)
    causal = rows < cols

    chunk = lambda dist, tl: jnp.maximum(tile(tl) - dist, 0)
    all_heads = tuple(range(nhead))

    def step(sc, sp, sw):
        cs = {}
        if sw is not None:
            dw, heads_w, slotw, tw = sw
            for s in heads_w:
                cs[s] = _dot(trit_ref[...], h_buf[slotw, s])
        if sc is not None:
            dc, heads_c, slotc, tc = sc
            k0 = pl.multiple_of(chunk(dc, tc) * tk, tk)
            for s in heads_c:
                za_buf[slotc, s] = _dot_nt(k_ref[0, s, pl.ds(k0, tk), :],
                                           q_ref[0, s, tc * tk:(tc + 1) * tk, :])
        if sp is not None:
            dp, heads_p, slotp, _ = sp
            for s in heads_p:
                zt = za_buf[slotp, s]
                v = jnp.maximum(zt, 0.0) + jnp.log(1.0 + jnp.exp(_neg_abs(zt)))
                zb_buf[slotp, s] = zt - v
                if isinstance(dp, int) and dp == 0:
                    v = jnp.where(causal, v, 0.0)
                h_buf[slotp, s] = v.astype(BF16)
        if sw is not None:
            cw = chunk(dw, tw)
            on = None if isinstance(dw, int) and dw == 0 else (dw <= tile(tw)).astype(F32)
            for s in heads_w:
                rs = tw * nhead + s
                r = r_buf[rs]
                p = jnp.exp(zb_buf[slotw, s] - cs[s])
                if on is None:
                    p = jnp.where(causal, p, 0.0)
                scale = jnp.exp(-r)
                add = cs[s][0:1, :] + h_buf[slotw, s, 0:1, :].astype(F32)
                if on is not None:
                    scale = scale * on
                    add = add * on
                acc_buf[rs] += _dot(vt_ref[0, s, cw], p.astype(BF16)) * scale
                r_buf[rs] = r + add

    def saturated(tl):
        return (jnp.min(r_buf[tl * nhead:(tl + 1) * nhead]) >= R_STOP).astype(jnp.int32)

    r_buf[...] = jnp.zeros_like(r_buf)
    acc_buf[...] = jnp.zeros_like(acc_buf)

    items = [(d, (s,), tl * N_NEAR + d, tl) for tl in range(ntile) for d in range(N_NEAR) for s in range(nhead)]
    at = lambda m: items[m] if 0 <= m < len(items) else None
    for m in range(len(items) + 2):
        step(at(m), at(m - 1), at(m - 2))

    for tl in range(ntile):
        far(tl, step, saturated, tile(tl), all_heads, sat_ref)

    for tl in range(ntile):
        for s in range(nhead):
            ot_ref[0, s, :, tl * tk:(tl + 1) * tk] = acc_buf[tl * nhead + s].astype(BF16)


def far(tl, step, saturated, qt, all_heads, sat_ref):
    s0, s1 = tl * N_NEAR + N_NEAR % 2, tl * N_NEAR + (N_NEAR + 1) % 2
    sat_ref[0] = saturated(tl)

    @pl.when(jnp.logical_and(qt >= N_NEAR, sat_ref[0] == 0))
    def _():
        step((N_NEAR, all_heads, s0, tl), None, None)
        step((N_NEAR + 1, all_heads, s1, tl), (N_NEAR, all_heads, s0, tl), None)

        def not_done(carry):
            j, done = carry
            return jnp.logical_and(N_NEAR + 2 * j <= qt, done == 0)

        def pair(carry):
            j, _ = carry
            d = N_NEAR + 2 * j
            step((d + 2, all_heads, s0, tl), (d + 1, all_heads, s1, tl), (d, all_heads, s0, tl))
            sat_ref[0] = saturated(tl)

            @pl.when(jnp.logical_and(d + 1 <= qt, sat_ref[0] == 0))
            def _():
                step((d + 3, all_heads, s1, tl), (d + 2, all_heads, s0, tl), (d + 1, all_heads, s1, tl))
                sat_ref[0] = saturated(tl)

            return j + 1, sat_ref[0]

        lax.while_loop(not_done, pair, (jnp.int32(0), jnp.int32(0)))


def _attention(q, k, vt, trit):
    b, nh, s, hd = q.shape
    tk = trit.shape[0]
    g = ATTN_HEADS_PER_STEP
    nt = min(ATTN_TILES_PER_STEP, s // tk)
    return pl.pallas_call(
        _attn_kernel,
        grid=(b, nh // g, s // (nt * tk)),
        in_specs=[
            pl.BlockSpec((1, g, nt * tk, hd), lambda i, h, j: (i, h, j, 0)),
            pl.BlockSpec((1, g, s, hd), lambda i, h, j: (i, h, 0, 0)),
            pl.BlockSpec((1, g, s // tk, hd, tk), lambda i, h, j: (i, h, 0, 0, 0)),
            pl.BlockSpec(trit.shape, lambda i, h, j: (0, 0)),
        ],
        out_specs=pl.BlockSpec((1, g, hd, nt * tk), lambda i, h, j: (i, h, 0, j)),
        out_shape=jax.ShapeDtypeStruct((b, nh, hd, s), BF16),
        scratch_shapes=[
            pltpu.VMEM((nt * N_NEAR, g, tk, tk), F32),
            pltpu.VMEM((nt * N_NEAR, g, tk, tk), F32),
            pltpu.VMEM((nt * N_NEAR, g, tk, tk), BF16),
            pltpu.VMEM((nt * g, 1, tk), F32),
            pltpu.VMEM((nt * g, hd, tk), F32),
            pltpu.SMEM((1,), jnp.int32),
        ],
        compiler_params=_cparams(("arbitrary", "arbitrary", "arbitrary"), ATTN_FLAGS),
        name="attn",
    )(q, k, vt, trit)


def _layer_norm(h, g, b):
    mu = jnp.mean(h, axis=-1, keepdims=True)
    hc = h - mu
    var = jnp.mean(hc * hc, axis=-1, keepdims=True)
    return hc * lax.rsqrt(var + LN_EPS) * g + b


def _first_argmax(vals):
    best = vals[0]
    idx = jnp.zeros(best.shape, jnp.int32)
    for j in range(1, len(vals)):
        upd = vals[j] > best
        best = jnp.where(upd, vals[j], best)
        idx = jnp.where(upd, j, idx)
    return best, idx


def _pick(idx, vals):
    out = vals[0]
    for j in range(1, len(vals)):
        out = jnp.where(idx == j, vals[j], out)
    return out


def _post_kernel(alpha, x_ref, g_ref, o_ref, mod_ref, wg_ref, wa_ref, wb_ref, wo_ref, lng_ref, lnb_ref,
                 wrt_ref, br_ref, su_ref,
                 x1_ref, u2_ref, ri_ref, rw_ref, cnt_ref, carry_ref):
    first = jnp.logical_and(pl.program_id(0) == 0, pl.program_id(1) == 0)

    @pl.when(first)
    def _():
        carry_ref[...] = jnp.zeros_like(carry_ref)

    d = x_ref.shape[2]
    x = x_ref[0]
    sh1 = mod_ref[0, 0:1, :]
    sc1 = mod_ref[0, 1:2, :]
    g1 = mod_ref[0, 2:3, :]
    sh2 = mod_ref[0, 3:4, :]
    sc2 = mod_ref[0, 4:5, :]
    u = (x * (1.0 + sc1) + sh1).astype(BF16)
    gates = _dot(u, wg_ref[...])
    y_a = _dot(g_ref[0], wa_ref[...])
    o_t = o_ref[0].reshape(ATTN_W, x.shape[0])
    y_b = lax.dot_general(o_t, wb_ref[...], (((0,), (0,)), ((), ())), preferred_element_type=F32)
    merged = jax.nn.sigmoid(gates[:, 0:d]) * y_a + jax.nn.sigmoid(gates[:, d:2 * d]) * y_b
    mix = _dot(merged.astype(BF16), wo_ref[...])
    x1 = _layer_norm(alpha * x + g1 * mix, lng_ref[...], lnb_ref[...])
    x1_ref[0] = x1
    u2 = x1 * (1.0 + sc2) + sh2
    u2_ref[0] = u2

    uh, ul = _split_bf16(u2)
    wh, wl = _split_bf16(wrt_ref[...])
    logits = _dot_nt(wh, uh) + _dot_nt(wh, ul) + _dot_nt(wl, uh)
    aff = jax.nn.sigmoid(logits)
    sel = aff + br_ref[...]
    selr = [sel[e:e + 1, :] for e in range(N_EXPERTS)]
    affr = [aff[e:e + 1, :] for e in range(N_EXPERTS)]
    gscore = []
    for gi in range(N_GROUPS):
        m = selr[gi * EXPERTS_PER_GROUP:(gi + 1) * EXPERTS_PER_GROUP]
        best = None
        for a in range(EXPERTS_PER_GROUP):
            for bb in range(a + 1, EXPERTS_PER_GROUP):
                pair = m[a] + m[bb]
                best = pair if best is None else jnp.maximum(best, pair)
        gscore.append(best)
    _, gidx = _first_argmax(gscore)
    sel_in = [_pick(gidx, [selr[gi * EXPERTS_PER_GROUP + j] for gi in range(N_GROUPS)]) for j in range(EXPERTS_PER_GROUP)]
    aff_in = [_pick(gidx, [affr[gi * EXPERTS_PER_GROUP + j] for gi in range(N_GROUPS)]) for j in range(EXPERTS_PER_GROUP)]
    _, i1 = _first_argmax(sel_in)
    neg = jnp.full_like(sel_in[0], -jnp.inf)
    _, i2 = _first_argmax([jnp.where(i1 == j, neg, sel_in[j]) for j in range(EXPERTS_PER_GROUP)])
    a1 = _pick(i1, aff_in)
    a2 = _pick(i2, aff_in)
    den = a1 + a2

    lo = jnp.minimum(i1, i2)
    hi = jnp.maximum(i1, i2)
    pair = jnp.where(lo == 0, hi - 1, jnp.where(lo == 1, hi + 1, len(PAIR_LO) - 1))
    cls = gidx * len(PAIR_LO) + pair
    w1 = a1 / den
    w2 = a2 / den
    first_is_lo = i1 < i2
    w_lo = jnp.where(first_is_lo, w1, w2)
    w_hi = jnp.where(first_is_lo, w2, w1)

    cio = lax.broadcasted_iota(jnp.int32, (CLASS_ROWS, cls.shape[1]), 0)
    hit = cio == cls
    member = jnp.where(hit, 1.0, 0.0)
    before = _dot(member.astype(BF16), su_ref[...]) + carry_ref[:, 0:1]
    rank = jnp.sum(jnp.where(hit, before, 0.0), axis=0, keepdims=True)
    t = member.shape[1]
    total = before[:, t - 1:t] + member[:, t - 1:t]
    carry_ref[...] = jnp.broadcast_to(total, carry_ref.shape)
    cnt_ref[...] = jnp.broadcast_to(total, cnt_ref.shape)

    zi = jnp.zeros_like(cls)
    ri_ref[0] = jnp.concatenate([cls, rank.astype(jnp.int32), zi, zi, zi, zi, zi, zi], axis=0)
    zf = jnp.zeros_like(w_lo)
    rw_ref[0] = jnp.concatenate([w_lo, w_hi, zf, zf, zf, zf, zf, zf], axis=0)


def _post(alpha, x, g, o, mod, w_g, w_a, w_b, w_o, ln_g, ln_b, w_rt, b_r, su):
    b, s, d = x.shape
    t = su.shape[0]
    nt = (b * s) // t
    tiles_per_b = s // t
    full2 = lambda a: pl.BlockSpec(a.shape, lambda i, j: (0, 0))
    tile_idx = lambda i, j: (i * tiles_per_b + j, 0, 0)
    return pl.pallas_call(
        functools.partial(_post_kernel, alpha),
        grid=(b, tiles_per_b),
        in_specs=[
            pl.BlockSpec((1, t, d), lambda i, j: (i, j, 0)),
            pl.BlockSpec((1, t, g.shape[2]), lambda i, j: (i, j, 0)),
            pl.BlockSpec((1, N_HEADS, HEAD_DIM, t), lambda i, j: (i, 0, 0, j)),
            pl.BlockSpec((1, 6, d), lambda i, j: (i, 0, 0)),
            full2(w_g), full2(w_a), full2(w_b), full2(w_o), full2(ln_g), full2(ln_b), full2(w_rt), full2(b_r), full2(su),
        ],
        out_specs=[
            pl.BlockSpec((1, t, d), lambda i, j: (i, j, 0)),
            pl.BlockSpec((1, t, d), lambda i, j: (i, j, 0)),
            pl.BlockSpec((1, 8, t), tile_idx),
            pl.BlockSpec((1, 8, t), tile_idx),
            pl.BlockSpec((CLASS_ROWS, 128), lambda i, j: (0, 0)),
        ],
        out_shape=[
            jax.ShapeDtypeStruct((b, s, d), F32),
            jax.ShapeDtypeStruct((b, s, d), F32),
            jax.ShapeDtypeStruct((nt, 8, t), jnp.int32),
            jax.ShapeDtypeStruct((nt, 8, t), F32),
            jax.ShapeDtypeStruct((CLASS_ROWS, 128), F32),
        ],
        scratch_shapes=[pltpu.VMEM((CLASS_ROWS, 128), F32)],
        compiler_params=_cparams(("arbitrary", "arbitrary")),
        name="post",
    )(x, g, o, mod, w_g, w_a, w_b, w_o, ln_g, ln_b, w_rt, b_r, su)


def _tile_row(ref, row):
    return ref.at[lax.shift_right_logical(row, 3), pl.ds(row & (SUBLANES - 1), 1)]


def _dispatch_kernel(last_ref, used_ref, nu_ref, dest_ref, u2_ref, xs_ref, zero_buf, sem, zsem):
    t = u2_ref.shape[0] * SUBLANES
    tm8 = zero_buf.shape[0]

    @pl.when(pl.program_id(0) == 0)
    def _():
        zero_buf[...] = jnp.zeros_like(zero_buf)

        def zero_block(block):
            cp = pltpu.make_async_copy(zero_buf, xs_ref.at[pl.ds(block * tm8, tm8)], zsem)
            cp.start()
            cp.wait()

        for e in range(N_CLASSES):
            @pl.when(used_ref[e] > 0)
            def _():
                zero_block(last_ref[e])

        def tail(b, c):
            zero_block(b)
            return c

        lax.fori_loop(nu_ref[0], xs_ref.shape[0] // tm8, tail, 0)

    def issue(g, c):
        for u in range(SUBLANES):
            dst = _tile_row(xs_ref, dest_ref[0, 0, g * SUBLANES + u])
            pltpu.make_async_copy(u2_ref.at[g, pl.ds(u, 1)], dst, sem).start(priority=u % 2)
        return c

    lax.fori_loop(0, t // SUBLANES, issue, 0)
    pltpu.make_async_copy(u2_ref, xs_ref.at[pl.ds(0, t // SUBLANES)], sem).wait()


def _dispatch(last_block, seg_used, n_used, dest, u2, p_rows):
    n8, _, d = u2.shape
    nt = dest.shape[0]
    t = n8 * SUBLANES // nt
    grid_spec = pltpu.PrefetchScalarGridSpec(
        num_scalar_prefetch=3,
        grid=(nt,),
        in_specs=[
            pl.BlockSpec((1, 1, t), lambda i, lb, us, nu: (i, 0, 0), memory_space=pltpu.SMEM),
            pl.BlockSpec((t // SUBLANES, SUBLANES, d), lambda i, lb, us, nu: (i, 0, 0)),
        ],
        out_specs=pl.BlockSpec(memory_space=pl.ANY),
        scratch_shapes=[pltpu.VMEM((TM // SUBLANES, SUBLANES, d), F32), pltpu.SemaphoreType.DMA(()),
                        pltpu.SemaphoreType.DMA(())],
    )
    return pl.pallas_call(
        _dispatch_kernel,
        grid_spec=grid_spec,
        out_shape=jax.ShapeDtypeStruct((p_rows // SUBLANES, SUBLANES, d), F32),
        compiler_params=_cparams(("arbitrary",)),
        name="dispatch",
    )(last_block, seg_used, n_used, dest, u2)


def _expert_kernel(lo_ref, hi_ref, nu_ref, xs_ref, wg1_ref, wu1_ref, wd1_ref, wg2_ref, wu2_ref, wd2_ref, y_ref):
    i = pl.program_id(0)
    d = xs_ref.shape[1]

    @pl.when(i < nu_ref[0])
    def _():
        xb = xs_ref[...].astype(BF16)
        for col, (wg_ref, wu_ref, wd_ref) in enumerate(((wg1_ref, wu1_ref, wd1_ref), (wg2_ref, wu2_ref, wd2_ref))):
            hg = _dot(xb, wg_ref[0])
            hu = _dot(xb, wu_ref[0])
            h = (hg * jax.nn.sigmoid(hg) * hu).astype(BF16)
            y_ref[:, col * d:(col + 1) * d] = _dot(h, wd_ref[0])

    @pl.when(i >= nu_ref[0])
    def _():
        y_ref[...] = jnp.zeros_like(y_ref)


def _experts(block_lo, block_hi, n_used, xs, w_gate, w_up, w_down):
    p, d = xs.shape
    ff = w_gate.shape[2]
    nb = p // TM
    w_in_spec = lambda which: pl.BlockSpec((1, d, ff), lambda i, lo, hi, nu: ((lo, hi)[which][i], 0, 0))
    w_out_spec = lambda which: pl.BlockSpec((1, ff, d), lambda i, lo, hi, nu: ((lo, hi)[which][i], 0, 0))
    grid_spec = pltpu.PrefetchScalarGridSpec(
        num_scalar_prefetch=3,
        grid=(nb,),
        in_specs=[
            pl.BlockSpec((TM, d), lambda i, lo, hi, nu: (jnp.minimum(i, nu[0] - 1), 0)),
            w_in_spec(0), w_in_spec(0), w_out_spec(0), w_in_spec(1), w_in_spec(1), w_out_spec(1),
        ],
        out_specs=pl.BlockSpec((TM, TOP_K * d), lambda i, lo, hi, nu: (i, 0)),
    )
    return pl.pallas_call(
        _expert_kernel,
        grid_spec=grid_spec,
        out_shape=jax.ShapeDtypeStruct((p, TOP_K * d), F32),
        compiler_params=_cparams(("arbitrary",)),
        name="experts",
    )(block_lo, block_hi, n_used, xs, w_gate, w_up, w_down, w_gate, w_up, w_down)


def _combine_kernel(alpha, dest_ref, dnext_ref, w_ref, x1_ref, mod_ref, lng_ref, lnb_ref, y_ref, o_ref, ybuf, sems):
    i = pl.program_id(0)
    t = x1_ref.shape[0]
    slot = lax.rem(i, 2)

    def gather(dref, sl):
        def issue(g, c):
            for u in range(SUBLANES):
                src = _tile_row(y_ref, dref[0, 0, g * SUBLANES + u])
                pltpu.make_async_copy(src, ybuf.at[sl, g, pl.ds(u, 1)], sems.at[sl]).start(priority=u % 2)
            return c

        lax.fori_loop(0, t // SUBLANES, issue, 0)

    @pl.when(i == 0)
    def _():
        gather(dest_ref, 0)

    @pl.when(i + 1 < pl.num_programs(0))
    def _():
        gather(dnext_ref, 1 - slot)

    pltpu.make_async_copy(y_ref.at[pl.ds(0, t // SUBLANES)], ybuf.at[slot], sems.at[slot]).wait()

    g2 = mod_ref[0, 5:6, :]
    d = x1_ref.shape[1]
    yy = ybuf[slot].reshape(t, TOP_K * d)
    ffn = w_ref[:, 0:1] * yy[:, 0:d] + w_ref[:, 1:2] * yy[:, d:2 * d]
    o_ref[...] = _layer_norm(alpha * x1_ref[...] + g2 * ffn, lng_ref[...], lnb_ref[...])


def _combine(alpha, dest, w, x1, mod, ln_g, ln_b, y, tiles_per_b):
    n, d = x1.shape
    nt = dest.shape[0]
    t = n // nt
    return pl.pallas_call(
        functools.partial(_combine_kernel, alpha),
        grid=(nt,),
        in_specs=[
            pl.BlockSpec((1, 1, t), lambda i: (i, 0, 0), memory_space=pltpu.SMEM),
            pl.BlockSpec((1, 1, t), lambda i: (jnp.minimum(i + 1, nt - 1), 0, 0), memory_space=pltpu.SMEM),
            pl.BlockSpec((t, 2), lambda i: (i, 0)),
            pl.BlockSpec((t, d), lambda i: (i, 0)),
            pl.BlockSpec((1, 6, d), lambda i: (i // tiles_per_b, 0, 0)),
            pl.BlockSpec(ln_g.shape, lambda i: (0, 0)),
            pl.BlockSpec(ln_b.shape, lambda i: (0, 0)),
            pl.BlockSpec(memory_space=pl.ANY),
        ],
        out_specs=pl.BlockSpec((t, d), lambda i: (i, 0)),
        out_shape=jax.ShapeDtypeStruct((n, d), F32),
        scratch_shapes=[pltpu.VMEM((2, t // SUBLANES, SUBLANES, TOP_K * d), F32), pltpu.SemaphoreType.DMA((2,))],
        compiler_params=_cparams(("arbitrary",)),
        name="combine",
    )(dest, dest, w, x1, mod, ln_g, ln_b, y)


def kernel(x, c, w_ada, b_ada, w_in, w_conv, w_a, w_b, w_o, ln1_g, ln1_b, w_router, b_router, w_gate, w_up, w_down,
           ln2_g, ln2_b):
    bsz, seq, d = x.shape
    depth = w_ada.shape[0]
    n = bsz * seq
    cch = w_conv.shape[2]
    alpha = float((2 * depth) ** 0.25)
    n_conv = 3 * cch
    n_qkv = 3 * ATTN_W

    c_pad = jnp.zeros((8, d), F32).at[:bsz].set(c)
    mod_all = _ada(c_pad, w_ada, b_ada)

    tk = min(TK, seq)
    ii = jnp.arange(tk, dtype=jnp.int32)
    trit = (ii[None, :] > ii[:, None]).astype(BF16)
    t_post = min(T_POST, seq)
    jj = jnp.arange(t_post, dtype=jnp.int32)
    su = (jj[:, None] < jj[None, :]).astype(BF16)
    w_rt = w_router.T
    b_r = b_router.reshape(N_EXPERTS, 1)

    n_blocks = n // TM + N_CLASSES
    p_rows = n_blocks * TM
    tiles_per_b = seq // t_post
    c_ids = jnp.arange(N_CLASSES, dtype=jnp.int32)
    pair_of = c_ids % len(PAIR_LO)
    cls_lo = (c_ids // len(PAIR_LO)) * EXPERTS_PER_GROUP + jnp.asarray(PAIR_LO, jnp.int32)[pair_of]
    cls_hi = (c_ids // len(PAIR_LO)) * EXPERTS_PER_GROUP + jnp.asarray(PAIR_HI, jnp.int32)[pair_of]

    for l in range(depth):
        mod = mod_all[l, :bsz].reshape(bsz, 6, d)
        w_in_l = w_in[l].astype(BF16)
        n_qk = 2 * ATTN_W
        g, q, k, vt = _inproj(x, mod, w_in_l[:, :n_conv], w_in_l[:, n_conv:n_conv + n_qk],
                              w_in_l[:, n_conv + n_qk:n_conv + n_qkv].T, w_conv[l], tk)
        o = _attention(q, k, vt, trit)
        x1, u2, ri, rw, cnt = _post(
            alpha, x, g, o, mod, w_in_l[:, n_conv + n_qkv:], w_a[l].astype(BF16), w_b[l].astype(BF16),
            w_o[l].astype(BF16), ln1_g[l].reshape(1, d), ln1_b[l].reshape(1, d), w_rt, b_r, su)

        counts = cnt[:N_CLASSES, 0].astype(jnp.int32)
        pad_counts = (counts + TM - 1) // TM * TM
        pad_end = jnp.cumsum(pad_counts)
        pad_start = pad_end - pad_counts
        cls = ri[:, 0:1, :]
        seg_start = jnp.sum(jnp.where(cls[..., None] == c_ids, pad_start, 0), axis=-1)
        dest = seg_start + ri[:, 1:2, :]
        block_start = jnp.arange(n_blocks, dtype=jnp.int32) * TM
        block_cls = jnp.minimum(jnp.sum((block_start[:, None] >= pad_end[None, :]).astype(jnp.int32), axis=1),
                                N_CLASSES - 1)
        block_lo = jnp.sum(jnp.where(block_cls[:, None] == c_ids, cls_lo, 0), axis=1)
        block_hi = jnp.sum(jnp.where(block_cls[:, None] == c_ids, cls_hi, 0), axis=1)
        n_used = (pad_end[N_CLASSES - 1] // TM).astype(jnp.int32).reshape(1)
        gate_w = jnp.transpose(rw[:, 0:2, :], (0, 2, 1)).reshape(n, 2)

        xs = _dispatch(pad_end // TM - 1, pad_counts, n_used, dest, u2.reshape(n // SUBLANES, SUBLANES, d), p_rows)
        y = _experts(block_lo, block_hi, n_used, xs.reshape(p_rows, d), w_gate[l].astype(BF16), w_up[l].astype(BF16),
                     w_down[l].astype(BF16))
        x = _combine(alpha, dest, gate_w, x1.reshape(n, d), mod, ln2_g[l].reshape(1, d), ln2_b[l].reshape(1, d),
                     y.reshape(p_rows // SUBLANES, SUBLANES, TOP_K * d), tiles_per_b).reshape(bsz, seq, d)
    return x
```

```python
import functools

import jax
import jax.numpy as jnp
from jax import lax
from jax.experimental import pallas as pl
from jax.experimental.pallas import tpu as pltpu

F32 = jnp.float32
BF16 = jnp.bfloat16

N_HEADS = 8
HEAD_DIM = 64
ATTN_W = N_HEADS * HEAD_DIM
N_EXPERTS = 16
N_GROUPS = 4
EXPERTS_PER_GROUP = N_EXPERTS // N_GROUPS
TOP_K = 2
PAIR_LO = (0, 0, 0, 1, 1, 2)
PAIR_HI = (1, 2, 3, 2, 3, 3)
N_CLASSES = N_GROUPS * len(PAIR_LO)
CLASS_ROWS = 32
CONV_K = 3
LN_EPS = 1e-5

T_INPROJ = 512
T_POST = 512
TK = 256
ATTN_HEADS_PER_STEP = 2
ATTN_TILES_PER_STEP = 4
N_NEAR = 3
TM = 512
SUBLANES = 8
ADA_TN = 1536

R_STOP = 110.0

VMEM_LIMIT = 56 * 1024 * 1024
ATTN_FLAGS = None


def _cparams(sem, flags=None):
    return pltpu.CompilerParams(dimension_semantics=sem, vmem_limit_bytes=VMEM_LIMIT, flags=flags)


def _dot(a, b):
    return jnp.dot(a, b, preferred_element_type=F32)


def _dot_nt(a, b):
    return lax.dot_general(a, b, (((1,), (1,)), ((), ())), preferred_element_type=F32)


def _split_bf16(v):
    hi = v.astype(BF16)
    lo = (v - hi.astype(F32)).astype(BF16)
    return hi, lo


def _ada_kernel(c_ref, w_ref, b_ref, o_ref):
    c = c_ref[...]
    s = c * jax.nn.sigmoid(c)
    o_ref[0] = jnp.dot(s, w_ref[0], precision=lax.Precision.HIGHEST, preferred_element_type=F32) + b_ref[0]


def _ada(c_pad, w_ada, b_ada):
    depth, d, n6 = w_ada.shape
    rows = c_pad.shape[0]
    return pl.pallas_call(
        _ada_kernel,
        grid=(depth, n6 // ADA_TN),
        in_specs=[
            pl.BlockSpec((rows, d), lambda l, j: (0, 0)),
            pl.BlockSpec((1, d, ADA_TN), lambda l, j: (l, 0, j)),
            pl.BlockSpec((1, 1, ADA_TN), lambda l, j: (l, 0, j)),
        ],
        out_specs=pl.BlockSpec((1, rows, ADA_TN), lambda l, j: (l, 0, j)),
        out_shape=jax.ShapeDtypeStruct((depth, rows, n6), F32),
        compiler_params=_cparams(("arbitrary", "arbitrary")),
        name="adaln",
    )(c_pad, w_ada, b_ada.reshape(depth, 1, n6))


def _inproj_kernel(x_ref, mod_ref, wc_ref, wqk_ref, wvt_ref, wconv_ref, g_ref, q_ref, k_ref, vt_ref, carry_ref):
    j = pl.program_id(1)
    t = x_ref.shape[1]
    cch = wconv_ref.shape[1]
    tk = vt_ref.shape[4]

    @pl.when(j == 0)
    def _():
        carry_ref[...] = jnp.zeros_like(carry_ref)

    sh1 = mod_ref[0, 0:1, :]
    sc1 = mod_ref[0, 1:2, :]
    u = (x_ref[0] * (1.0 + sc1) + sh1).astype(BF16)

    pc = _dot(u, wc_ref[...])
    cb = pc[:, 0:cch]
    h = pc[:, cch:2 * cch] * pc[:, 2 * cch:3 * cch]
    prev = carry_ref[...]
    p1 = prev[7:8, :]
    p2 = prev[6:7, :]
    row = lax.broadcasted_iota(jnp.int32, h.shape, 0)
    hm1 = jnp.where(row == 0, p1, pltpu.roll(h, 1, 0))
    hm2 = jnp.where(row == 0, p2, jnp.where(row == 1, p1, pltpu.roll(h, 2, 0)))
    wcv = wconv_ref[...]
    conv = wcv[2:3, :] * h + wcv[1:2, :] * hm1 + wcv[0:1, :] * hm2
    g_ref[0] = (cb * conv).astype(BF16)
    carry_ref[...] = h[t - 8:t, :]

    qk = _dot(u, wqk_ref[...])
    scale = HEAD_DIM ** -0.5
    for hd in range(N_HEADS):
        lo = hd * HEAD_DIM
        q_ref[0, hd] = (qk[:, lo:lo + HEAD_DIM] * scale).astype(BF16)
        k_ref[0, hd] = qk[:, ATTN_W + lo:ATTN_W + lo + HEAD_DIM].astype(BF16)
    vt = _dot_nt(wvt_ref[...], u)
    for hd in range(N_HEADS):
        for cj in range(t // tk):
            vt_ref[0, hd, cj] = vt[hd * HEAD_DIM:(hd + 1) * HEAD_DIM, cj * tk:(cj + 1) * tk].astype(BF16)


def _inproj(x, mod, w_c, w_qk, w_vt, w_conv, tk):
    b, s, d = x.shape
    cch = w_conv.shape[1]
    t = min(T_INPROJ, s)
    hshape = jax.ShapeDtypeStruct((b, N_HEADS, s, HEAD_DIM), BF16)
    hspec = pl.BlockSpec((1, N_HEADS, t, HEAD_DIM), lambda i, j: (i, 0, j, 0))
    return pl.pallas_call(
        _inproj_kernel,
        grid=(b, s // t),
        in_specs=[
            pl.BlockSpec((1, t, d), lambda i, j: (i, j, 0)),
            pl.BlockSpec((1, 6, d), lambda i, j: (i, 0, 0)),
            pl.BlockSpec(w_c.shape, lambda i, j: (0, 0)),
            pl.BlockSpec(w_qk.shape, lambda i, j: (0, 0)),
            pl.BlockSpec(w_vt.shape, lambda i, j: (0, 0)),
            pl.BlockSpec(w_conv.shape, lambda i, j: (0, 0)),
        ],
        out_specs=[
            pl.BlockSpec((1, t, cch), lambda i, j: (i, j, 0)), hspec, hspec,
            pl.BlockSpec((1, N_HEADS, t // tk, HEAD_DIM, tk), lambda i, j: (i, 0, j, 0, 0)),
        ],
        out_shape=[
            jax.ShapeDtypeStruct((b, s, cch), BF16), hshape, hshape,
            jax.ShapeDtypeStruct((b, N_HEADS, s // tk, HEAD_DIM, tk), BF16),
        ],
        scratch_shapes=[pltpu.VMEM((8, cch), F32)],
        compiler_params=_cparams(("arbitrary", "arbitrary")),
        name="inproj",
    )(x, mod, w_c, w_qk, w_vt, w_conv)


def _neg_abs(z):
    bits = lax.bitcast_convert_type(z, jnp.uint32) | jnp.uint32(0x80000000)
    return lax.bitcast_convert_type(bits, F32)


def _attn_kernel(q_ref, k_ref, vt_ref, trit_ref, ot_ref, za_buf, zb_buf, h_buf, r_buf, acc_buf, sat_ref):
    tk = za_buf.shape[2]
    nhead = za_buf.shape[1]
    ntile = za_buf.shape[0] // N_NEAR
    tile = lambda tl: pl.program_id(2) * ntile + tl
    rows = lax.broadcasted_iota(jnp.int32, (tk, tk), 0)
    cols = lax.broadcasted_iota(jnp.int32, (tk, tk), 1)
    causal = rows < cols

    chunk = lambda dist, tl: jnp.maximum(tile(tl) - dist, 0)
    all_heads = tuple(range(nhead))

    def step(sc, sp, sw):
        cs = {}
        if sw is not None:
            dw, heads_w, slotw, tw = sw
            for s in heads_w:
                cs[s] = _dot(trit_ref[...], h_buf[slotw, s])
        if sc is not None:
            dc, heads_c, slotc, tc = sc
            k0 = pl.multiple_of(chunk(dc, tc) * tk, tk)
            for s in heads_c:
                za_buf[slotc, s] = _dot_nt(k_ref[0, s, pl.ds(k0, tk), :],
                                           q_ref[0, s, tc * tk:(tc + 1) * tk, :])
        if sp is not None:
            dp, heads_p, slotp, _ = sp
            for s in heads_p:
                zt = za_buf[slotp, s]
                v = jnp.maximum(zt, 0.0) + jnp.log(1.0 + jnp.exp(_neg_abs(zt)))
                zb_buf[slotp, s] = zt - v
                if isinstance(dp, int) and dp == 0:
                    v = jnp.where(causal, v, 0.0)
                h_buf[slotp, s] = v.astype(BF16)
        if sw is not None:
            cw = chunk(dw, tw)
            on = None if isinstance(dw, int) and dw == 0 else (dw <= tile(tw)).astype(F32)
            for s in heads_w:
                rs = tw * nhead + s
                r = r_buf[rs]
                p = jnp.exp(zb_buf[slotw, s] - cs[s])
                if on is None:
                    p = jnp.where(causal, p, 0.0)
                scale = jnp.exp(-r)
                add = cs[s][0:1, :] + h_buf[slotw, s, 0:1, :].astype(F32)
                if on is not None:
                    scale = scale * on
                    add = add * on
                acc_buf[rs] += _dot(vt_ref[0, s, cw], p.astype(BF16)) * scale
                r_buf[rs] = r + add

    def saturated(tl):
        return (jnp.min(r_buf[tl * nhead:(tl + 1) * nhead]) >= R_STOP).astype(jnp.int32)

    r_buf[...] = jnp.zeros_like(r_buf)
    acc_buf[...] = jnp.zeros_like(acc_buf)

    items = [(d, (s,), tl * N_NEAR + d, tl) for tl in range(ntile) for d in range(N_NEAR) for s in range(nhead)]
    at = lambda m: items[m] if 0 <= m < len(items) else None
    for m in range(len(items) + 2):
        step(at(m), at(m - 1), at(m - 2))

    for tl in range(ntile):
        far(tl, step, saturated, tile(tl), all_heads, sat_ref)

    for tl in range(ntile):
        for s in range(nhead):
            ot_ref[0, s, :, tl * tk:(tl + 1) * tk] = acc_buf[tl * nhead + s].astype(BF16)


def far(tl, step, saturated, qt, all_heads, sat_ref):
    s0, s1 = tl * N_NEAR + N_NEAR % 2, tl * N_NEAR + (N_NEAR + 1) % 2
    sat_ref[0] = saturated(tl)

    @pl.when(jnp.logical_and(qt >= N_NEAR, sat_ref[0] == 0))
    def _():
        step((N_NEAR, all_heads, s0, tl), None, None)
        step((N_NEAR + 1, all_heads, s1, tl), (N_NEAR, all_heads, s0, tl), None)

        def not_done(carry):
            j, done = carry
            return jnp.logical_and(N_NEAR + 2 * j <= qt, done == 0)

        def pair(carry):
            j, _ = carry
            d = N_NEAR + 2 * j
            step((d + 2, all_heads, s0, tl), (d + 1, all_heads, s1, tl), (d, all_heads, s0, tl))
            sat_ref[0] = saturated(tl)

            @pl.when(jnp.logical_and(d + 1 <= qt, sat_ref[0] == 0))
            def _():
                step((d + 3, all_heads, s1, tl), (d + 2, all_heads, s0, tl), (d + 1, all_heads, s1, tl))
                sat_ref[0] = saturated(tl)

            return j + 1, sat_ref[0]

        lax.while_loop(not_done, pair, (jnp.int32(0), jnp.int32(0)))


def _attention(q, k, vt, trit):
    b, nh, s, hd = q.shape
    tk = trit.shape[0]
    g = ATTN_HEADS_PER_STEP
    nt = min(ATTN_TILES_PER_STEP, s // tk)
    return pl.pallas_call(
        _attn_kernel,
        grid=(b, nh // g, s // (nt * tk)),
        in_specs=[
            pl.BlockSpec((1, g, nt * tk, hd), lambda i, h, j: (i, h, j, 0)),
            pl.BlockSpec((1, g, s, hd), lambda i, h, j: (i, h, 0, 0)),
            pl.BlockSpec((1, g, s // tk, hd, tk), lambda i, h, j: (i, h, 0, 0, 0)),
            pl.BlockSpec(trit.shape, lambda i, h, j: (0, 0)),
        ],
        out_specs=pl.BlockSpec((1, g, hd, nt * tk), lambda i, h, j: (i, h, 0, j)),
        out_shape=jax.ShapeDtypeStruct((b, nh, hd, s), BF16),
        scratch_shapes=[
            pltpu.VMEM((nt * N_NEAR, g, tk, tk), F32),
            pltpu.VMEM((nt * N_NEAR, g, tk, tk), F32),
            pltpu.VMEM((nt * N_NEAR, g, tk, tk), BF16),
            pltpu.VMEM((nt * g, 1, tk), F32),
            pltpu.VMEM((nt * g, hd, tk), F32),
            pltpu.SMEM((1,), jnp.int32),
        ],
        compiler_params=_cparams(("arbitrary", "arbitrary", "arbitrary"), ATTN_FLAGS),
        name="attn",
    )(q, k, vt, trit)


def _layer_norm(h, g, b):
    mu = jnp.mean(h, axis=-1, keepdims=True)
    hc = h - mu
    var = jnp.mean(hc * hc, axis=-1, keepdims=True)
    return hc * lax.rsqrt(var + LN_EPS) * g + b


def _first_argmax(vals):
    best = vals[0]
    idx = jnp.zeros(best.shape, jnp.int32)
    for j in range(1, len(vals)):
        upd = vals[j] > best
        best = jnp.where(upd, vals[j], best)
        idx = jnp.where(upd, j, idx)
    return best, idx


def _pick(idx, vals):
    out = vals[0]
    for j in range(1, len(vals)):
        out = jnp.where(idx == j, vals[j], out)
    return out


def _post_kernel(alpha, x_ref, g_ref, o_ref, mod_ref, wg_ref, wa_ref, wb_ref, wo_ref, lng_ref, lnb_ref,
                 wrt_ref, br_ref, su_ref,
                 x1_ref, u2_ref, ri_ref, rw_ref, cnt_ref, carry_ref):
    first = jnp.logical_and(pl.program_id(0) == 0, pl.program_id(1) == 0)

    @pl.when(first)
    def _():
        carry_ref[...] = jnp.zeros_like(carry_ref)

    d = x_ref.shape[2]
    x = x_ref[0]
    sh1 = mod_ref[0, 0:1, :]
    sc1 = mod_ref[0, 1:2, :]
    g1 = mod_ref[0, 2:3, :]
    sh2 = mod_ref[0, 3:4, :]
    sc2 = mod_ref[0, 4:5, :]
    u = (x * (1.0 + sc1) + sh1).astype(BF16)
    gates = _dot(u, wg_ref[...])
    y_a = _dot(g_ref[0], wa_ref[...])
    o_t = o_ref[0].reshape(ATTN_W, x.shape[0])
    y_b = lax.dot_general(o_t, wb_ref[...], (((0,), (0,)), ((), ())), preferred_element_type=F32)
    merged = jax.nn.sigmoid(gates[:, 0:d]) * y_a + jax.nn.sigmoid(gates[:, d:2 * d]) * y_b
    mix = _dot(merged.astype(BF16), wo_ref[...])
    x1 = _layer_norm(alpha * x + g1 * mix, lng_ref[...], lnb_ref[...])
    x1_ref[0] = x1
    u2 = x1 * (1.0 + sc2) + sh2
    u2_ref[0] = u2

    uh, ul = _split_bf16(u2)
    wh, wl = _split_bf16(wrt_ref[...])
    logits = _dot_nt(wh, uh) + _dot_nt(wh, ul) + _dot_nt(wl, uh)
    aff = jax.nn.sigmoid(logits)
    sel = aff + br_ref[...]
    selr = [sel[e:e + 1, :] for e in range(N_EXPERTS)]
    affr = [aff[e:e + 1, :] for e in range(N_EXPERTS)]
    gscore = []
    for gi in range(N_GROUPS):
        m = selr[gi * EXPERTS_PER_GROUP:(gi + 1) * EXPERTS_PER_GROUP]
        best = None
        for a in range(EXPERTS_PER_GROUP):
            for bb in range(a + 1, EXPERTS_PER_GROUP):
                pair = m[a] + m[bb]
                best = pair if best is None else jnp.maximum(best, pair)
        gscore.append(best)
    _, gidx = _first_argmax(gscore)
    sel_in = [_pick(gidx, [selr[gi * EXPERTS_PER_GROUP + j] for gi in range(N_GROUPS)]) for j in range(EXPERTS_PER_GROUP)]
    aff_in = [_pick(gidx, [affr[gi * EXPERTS_PER_GROUP + j] for gi in range(N_GROUPS)]) for j in range(EXPERTS_PER_GROUP)]
    _, i1 = _first_argmax(sel_in)
    neg = jnp.full_like(sel_in[0], -jnp.inf)
    _, i2 = _first_argmax([jnp.where(i1 == j, neg, sel_in[j]) for j in range(EXPERTS_PER_GROUP)])
    a1 = _pick(i1, aff_in)
    a2 = _pick(i2, aff_in)
    den = a1 + a2

    lo = jnp.minimum(i1, i2)
    hi = jnp.maximum(i1, i2)
    pair = jnp.where(lo == 0, hi - 1, jnp.where(lo == 1, hi + 1, len(PAIR_LO) - 1))
    cls = gidx * len(PAIR_LO) + pair
    w1 = a1 / den
    w2 = a2 / den
    first_is_lo = i1 < i2
    w_lo = jnp.where(first_is_lo, w1, w2)
    w_hi = jnp.where(first_is_lo, w2, w1)

    cio = lax.broadcasted_iota(jnp.int32, (CLASS_ROWS, cls.shape[1]), 0)
    hit = cio == cls
    member = jnp.where(hit, 1.0, 0.0)
    before = _dot(member.astype(BF16), su_ref[...]) + carry_ref[:, 0:1]
    rank = jnp.sum(jnp.where(hit, before, 0.0), axis=0, keepdims=True)
    t = member.shape[1]
    total = before[:, t - 1:t] + member[:, t - 1:t]
    carry_ref[...] = jnp.broadcast_to(total, carry_ref.shape)
    cnt_ref[...] = jnp.broadcast_to(total, cnt_ref.shape)

    zi = jnp.zeros_like(cls)
    ri_ref[0] = jnp.concatenate([cls, rank.astype(jnp.int32), zi, zi, zi, zi, zi, zi], axis=0)
    zf = jnp.zeros_like(w_lo)
    rw_ref[0] = jnp.concatenate([w_lo, w_hi, zf, zf, zf, zf, zf, zf], axis=0)


def _post(alpha, x, g, o, mod, w_g, w_a, w_b, w_o, ln_g, ln_b, w_rt, b_r, su):
    b, s, d = x.shape
    t = su.shape[0]
    nt = (b * s) // t
    tiles_per_b = s // t
    full2 = lambda a: pl.BlockSpec(a.shape, lambda i, j: (0, 0))
    tile_idx = lambda i, j: (i * tiles_per_b + j, 0, 0)
    return pl.pallas_call(
        functools.partial(_post_kernel, alpha),
        grid=(b, tiles_per_b),
        in_specs=[
            pl.BlockSpec((1, t, d), lambda i, j: (i, j, 0)),
            pl.BlockSpec((1, t, g.shape[2]), lambda i, j: (i, j, 0)),
            pl.BlockSpec((1, N_HEADS, HEAD_DIM, t), lambda i, j: (i, 0, 0, j)),
            pl.BlockSpec((1, 6, d), lambda i, j: (i, 0, 0)),
            full2(w_g), full2(w_a), full2(w_b), full2(w_o), full2(ln_g), full2(ln_b), full2(w_rt), full2(b_r), full2(su),
        ],
        out_specs=[
            pl.BlockSpec((1, t, d), lambda i, j: (i, j, 0)),
            pl.BlockSpec((1, t, d), lambda i, j: (i, j, 0)),
            pl.BlockSpec((1, 8, t), tile_idx),
            pl.BlockSpec((1, 8, t), tile_idx),
            pl.BlockSpec((CLASS_ROWS, 128), lambda i, j: (0, 0)),
        ],
        out_shape=[
            jax.ShapeDtypeStruct((b, s, d), F32),
            jax.ShapeDtypeStruct((b, s, d), F32),
            jax.ShapeDtypeStruct((nt, 8, t), jnp.int32),
            jax.ShapeDtypeStruct((nt, 8, t), F32),
            jax.ShapeDtypeStruct((CLASS_ROWS, 128), F32),
        ],
        scratch_shapes=[pltpu.VMEM((CLASS_ROWS, 128), F32)],
        compiler_params=_cparams(("arbitrary", "arbitrary")),
        name="post",
    )(x, g, o, mod, w_g, w_a, w_b, w_o, ln_g, ln_b, w_rt, b_r, su)


def _tile_row(ref, row):
    return ref.at[lax.shift_right_logical(row, 3), pl.ds(row & (SUBLANES - 1), 1)]


def _dispatch_kernel(last_ref, used_ref, nu_ref, dest_ref, u2_ref, xs_ref, zero_buf, sem, zsem):
    t = u2_ref.shape[0] * SUBLANES
    tm8 = zero_buf.shape[0]

    @pl.when(pl.program_id(0) == 0)
    def _():
        zero_buf[...] = jnp.zeros_like(zero_buf)

        def zero_block(block):
            cp = pltpu.make_async_copy(zero_buf, xs_ref.at[pl.ds(block * tm8, tm8)], zsem)
            cp.start()
            cp.wait()

        for e in range(N_CLASSES):
            @pl.when(used_ref[e] > 0)
            def _():
                zero_block(last_ref[e])

        def tail(b, c):
            zero_block(b)
            return c

        lax.fori_loop(nu_ref[0], xs_ref.shape[0] // tm8, tail, 0)

    def issue(g, c):
        for u in range(SUBLANES):
            dst = _tile_row(xs_ref, dest_ref[0, 0, g * SUBLANES + u])
            pltpu.make_async_copy(u2_ref.at[g, pl.ds(u, 1)], dst, sem).start(priority=u % 2)
        return c

    lax.fori_loop(0, t // SUBLANES, issue, 0)
    pltpu.make_async_copy(u2_ref, xs_ref.at[pl.ds(0, t // SUBLANES)], sem).wait()


def _dispatch(last_block, seg_used, n_used, dest, u2, p_rows):
    n8, _, d = u2.shape
    nt = dest.shape[0]
    t = n8 * SUBLANES // nt
    grid_spec = pltpu.PrefetchScalarGridSpec(
        num_scalar_prefetch=3,
        grid=(nt,),
        in_specs=[
            pl.BlockSpec((1, 1, t), lambda i, lb, us, nu: (i, 0, 0), memory_space=pltpu.SMEM),
            pl.BlockSpec((t // SUBLANES, SUBLANES, d), lambda i, lb, us, nu: (i, 0, 0)),
        ],
        out_specs=pl.BlockSpec(memory_space=pl.ANY),
        scratch_shapes=[pltpu.VMEM((TM // SUBLANES, SUBLANES, d), F32), pltpu.SemaphoreType.DMA(()),
                        pltpu.SemaphoreType.DMA(())],
    )
    return pl.pallas_call(
        _dispatch_kernel,
        grid_spec=grid_spec,
        out_shape=jax.ShapeDtypeStruct((p_rows // SUBLANES, SUBLANES, d), F32),
        compiler_params=_cparams(("arbitrary",)),
        name="dispatch",
    )(last_block, seg_used, n_used, dest, u2)


def _expert_kernel(lo_ref, hi_ref, nu_ref, xs_ref, wg1_ref, wu1_ref, wd1_ref, wg2_ref, wu2_ref, wd2_ref, y_ref):
    i = pl.program_id(0)
    d = xs_ref.shape[1]

    @pl.when(i < nu_ref[0])
    def _():
        xb = xs_ref[...].astype(BF16)
        for col, (wg_ref, wu_ref, wd_ref) in enumerate(((wg1_ref, wu1_ref, wd1_ref), (wg2_ref, wu2_ref, wd2_ref))):
            hg = _dot(xb, wg_ref[0])
            hu = _dot(xb, wu_ref[0])
            h = (hg * jax.nn.sigmoid(hg) * hu).astype(BF16)
            y_ref[:, col * d:(col + 1) * d] = _dot(h, wd_ref[0])

    @pl.when(i >= nu_ref[0])
    def _():
        y_ref[...] = jnp.zeros_like(y_ref)


def _experts(block_lo, block_hi, n_used, xs, w_gate, w_up, w_down):
    p, d = xs.shape
    ff = w_gate.shape[2]
    nb = p // TM
    w_in_spec = lambda which: pl.BlockSpec((1, d, ff), lambda i, lo, hi, nu: ((lo, hi)[which][i], 0, 0))
    w_out_spec = lambda which: pl.BlockSpec((1, ff, d), lambda i, lo, hi, nu: ((lo, hi)[which][i], 0, 0))
    grid_spec = pltpu.PrefetchScalarGridSpec(
        num_scalar_prefetch=3,
        grid=(nb,),
        in_specs=[
            pl.BlockSpec((TM, d), lambda i, lo, hi, nu: (jnp.minimum(i, nu[0] - 1), 0)),
            w_in_spec(0), w_in_spec(0), w_out_spec(0), w_in_spec(1), w_in_spec(1), w_out_spec(1),
        ],
        out_specs=pl.BlockSpec((TM, TOP_K * d), lambda i, lo, hi, nu: (i, 0)),
    )
    return pl.pallas_call(
        _expert_kernel,
        grid_spec=grid_spec,
        out_shape=jax.ShapeDtypeStruct((p, TOP_K * d), F32),
        compiler_params=_cparams(("arbitrary",)),
        name="experts",
    )(block_lo, block_hi, n_used, xs, w_gate, w_up, w_down, w_gate, w_up, w_down)


def _combine_kernel(alpha, dest_ref, dnext_ref, w_ref, x1_ref, mod_ref, lng_ref, lnb_ref, y_ref, o_ref, ybuf, sems):
    i = pl.program_id(0)
    t = x1_ref.shape[0]
    slot = lax.rem(i, 2)

    def gather(dref, sl):
        def issue(g, c):
            for u in range(SUBLANES):
                src = _tile_row(y_ref, dref[0, 0, g * SUBLANES + u])
                pltpu.make_async_copy(src, ybuf.at[sl, g, pl.ds(u, 1)], sems.at[sl]).start(priority=u % 2)
            return c

        lax.fori_loop(0, t // SUBLANES, issue, 0)

    @pl.when(i == 0)
    def _():
        gather(dest_ref, 0)

    @pl.when(i + 1 < pl.num_programs(0))
    def _():
        gather(dnext_ref, 1 - slot)

    pltpu.make_async_copy(y_ref.at[pl.ds(0, t // SUBLANES)], ybuf.at[slot], sems.at[slot]).wait()

    g2 = mod_ref[0, 5:6, :]
    d = x1_ref.shape[1]
    yy = ybuf[slot].reshape(t, TOP_K * d)
    ffn = w_ref[:, 0:1] * yy[:, 0:d] + w_ref[:, 1:2] * yy[:, d:2 * d]
    o_ref[...] = _layer_norm(alpha * x1_ref[...] + g2 * ffn, lng_ref[...], lnb_ref[...])


def _combine(alpha, dest, w, x1, mod, ln_g, ln_b, y, tiles_per_b):
    n, d = x1.shape
    nt = dest.shape[0]
    t = n // nt
    return pl.pallas_call(
        functools.partial(_combine_kernel, alpha),
        grid=(nt,),
        in_specs=[
            pl.BlockSpec((1, 1, t), lambda i: (i, 0, 0), memory_space=pltpu.SMEM),
            pl.BlockSpec((1, 1, t), lambda i: (jnp.minimum(i + 1, nt - 1), 0, 0), memory_space=pltpu.SMEM),
            pl.BlockSpec((t, 2), lambda i: (i, 0)),
            pl.BlockSpec((t, d), lambda i: (i, 0)),
            pl.BlockSpec((1, 6, d), lambda i: (i // tiles_per_b, 0, 0)),
            pl.BlockSpec(ln_g.shape, lambda i: (0, 0)),
            pl.BlockSpec(ln_b.shape, lambda i: (0, 0)),
            pl.BlockSpec(memory_space=pl.ANY),
        ],
        out_specs=pl.BlockSpec((t, d), lambda i: (i, 0)),
        out_shape=jax.ShapeDtypeStruct((n, d), F32),
        scratch_shapes=[pltpu.VMEM((2, t // SUBLANES, SUBLANES, TOP_K * d), F32), pltpu.SemaphoreType.DMA((2,))],
        compiler_params=_cparams(("arbitrary",)),
        name="combine",
    )(dest, dest, w, x1, mod, ln_g, ln_b, y)


def kernel(x, c, w_ada, b_ada, w_in, w_conv, w_a, w_b, w_o, ln1_g, ln1_b, w_router, b_router, w_gate, w_up, w_down,
           ln2_g, ln2_b):
    bsz, seq, d = x.shape
    depth = w_ada.shape[0]
    n = bsz * seq
    cch = w_conv.shape[2]
    alpha = float((2 * depth) ** 0.25)
    n_conv = 3 * cch
    n_qkv = 3 * ATTN_W

    c_pad = jnp.zeros((8, d), F32).at[:bsz].set(c)
    mod_all = _ada(c_pad, w_ada, b_ada)

    tk = min(TK, seq)
    ii = jnp.arange(tk, dtype=jnp.int32)
    trit = (ii[None, :] > ii[:, None]).astype(BF16)
    t_post = min(T_POST, seq)
    jj = jnp.arange(t_post, dtype=jnp.int32)
    su = (jj[:, None] < jj[None, :]).astype(BF16)
    w_rt = w_router.T
    b_r = b_router.reshape(N_EXPERTS, 1)

    n_blocks = n // TM + N_CLASSES
    p_rows = n_blocks * TM
    tiles_per_b = seq // t_post
    c_ids = jnp.arange(N_CLASSES, dtype=jnp.int32)
    pair_of = c_ids % len(PAIR_LO)
    cls_lo = (c_ids // len(PAIR_LO)) * EXPERTS_PER_GROUP + jnp.asarray(PAIR_LO, jnp.int32)[pair_of]
    cls_hi = (c_ids // len(PAIR_LO)) * EXPERTS_PER_GROUP + jnp.asarray(PAIR_HI, jnp.int32)[pair_of]

    for l in range(depth):
        mod = mod_all[l, :bsz].reshape(bsz, 6, d)
        w_in_l = w_in[l].astype(BF16)
        n_qk = 2 * ATTN_W
        g, q, k, vt = _inproj(x, mod, w_in_l[:, :n_conv], w_in_l[:, n_conv:n_conv + n_qk],
                              w_in_l[:, n_conv + n_qk:n_conv + n_qkv].T, w_conv[l], tk)
        o = _attention(q, k, vt, trit)
        x1, u2, ri, rw, cnt = _post(
            alpha, x, g, o, mod, w_in_l[:, n_conv + n_qkv:], w_a[l].astype(BF16), w_b[l].astype(BF16),
            w_o[l].astype(BF16), ln1_g[l].reshape(1, d), ln1_b[l].reshape(1, d), w_rt, b_r, su)

        counts = cnt[:N_CLASSES, 0].astype(jnp.int32)
        pad_counts = (counts + TM - 1) // TM * TM
        pad_end = jnp.cumsum(pad_counts)
        pad_start = pad_end - pad_counts
        cls = ri[:, 0:1, :]
        seg_start = jnp.sum(jnp.where(cls[..., None] == c_ids, pad_start, 0), axis=-1)
        dest = seg_start + ri[:, 1:2, :]
        block_start = jnp.arange(n_blocks, dtype=jnp.int32) * TM
        block_cls = jnp.minimum(jnp.sum((block_start[:, None] >= pad_end[None, :]).astype(jnp.int32), axis=1),
                                N_CLASSES - 1)
        block_lo = jnp.sum(jnp.where(block_cls[:, None] == c_ids, cls_lo, 0), axis=1)
        block_hi = jnp.sum(jnp.where(block_cls[:, None] == c_ids, cls_hi, 0), axis=1)
        n_used = (pad_end[N_CLASSES - 1] // TM).astype(jnp.int32).reshape(1)
        gate_w = jnp.transpose(rw[:, 0:2, :], (0, 2, 1)).reshape(n, 2)

        xs = _dispatch(pad_end // TM - 1, pad_counts, n_used, dest, u2.reshape(n // SUBLANES, SUBLANES, d), p_rows)
        y = _experts(block_lo, block_hi, n_used, xs.reshape(p_rows, d), w_gate[l].astype(BF16), w_up[l].astype(BF16),
                     w_down[l].astype(BF16))
        x = _combine(alpha, dest, gate_w, x1.reshape(n, d), mod, ln2_g[l].reshape(1, d), ln2_b[l].reshape(1, d),
                     y.reshape(p_rows // SUBLANES, SUBLANES, TOP_K * d), tiles_per_b).reshape(bsz, seq, d)
    return x
```
